```python
import math, functools
import jax, jax.numpy as jnp
from jax import lax
import numpy as np

D_MODEL = 1024
BATCH = 16
SEQ = 2048
DEPTH = 2
DEC_BATCH = 32
DEC_SEQ = 4
PAST_LEN = 16384
PAGE_SIZE = 128

GDN_HEADS = D_MODEL // 256
GDN_DK = 128
GDN_DV = 128
GDN_CONV = 4
GDN_CHUNK = 64
GDN_QK = GDN_HEADS * GDN_DK
GDN_V = GDN_HEADS * GDN_DV
GDN_CONV_CH = 2 * GDN_QK + GDN_V
SC_WIDTH = D_MODEL // 2
SC_CONV = 3
DIFF_HEADS = D_MODEL // 256
DIFF_HD = 64
DIFF_VD = 2 * DIFF_HD
DIFF_QK = DIFF_HEADS * 2 * DIFF_HD
DIFF_V = DIFF_HEADS * DIFF_VD
DIFF_SCALE = DIFF_HD ** -0.5
Q_BLOCK = 128
N_BRANCH = 3
BRANCH_W = D_MODEL // 2
D_FF = 4 * D_MODEL
DN_ALPHA = (2 * DEPTH) ** 0.25
DN_BETA = (8 * DEPTH) ** -0.25
LN_EPS = 1e-5
RMS_EPS = 1e-6
SPLIT_SIZES = (GDN_CONV_CH, GDN_V, GDN_HEADS, GDN_HEADS, SC_WIDTH, SC_WIDTH, SC_WIDTH, DIFF_QK, DIFF_QK, DIFF_V, N_BRANCH * D_MODEL)
IN_COLS = sum(SPLIT_SIZES)
SPLIT_OFFSETS = tuple(int(v) for v in np.cumsum(SPLIT_SIZES)[:-1])

kernel_name = 'gdn_shortconv_diffattn_parallel_hybrid_step'


def layer_norm(x, g, b):
    xf = x.astype(jnp.float32)
    mu = jnp.mean(xf, axis=-1, keepdims=True)
    var = jnp.mean(jnp.square(xf - mu), axis=-1, keepdims=True)
    return ((xf - mu) * lax.rsqrt(var + LN_EPS) * g + b).astype(x.dtype)


def rms_norm(x, w):
    xf = x.astype(jnp.float32)
    return (xf * lax.rsqrt(jnp.mean(jnp.square(xf), axis=-1, keepdims=True) + RMS_EPS) * w).astype(x.dtype)


def l2_norm(x):
    return x * lax.rsqrt(jnp.sum(jnp.square(x), axis=-1, keepdims=True) + RMS_EPS)


def causal_conv(u, buf, w):
    width = w.shape[0]
    l = u.shape[1]
    up = jnp.concatenate([buf.astype(u.dtype), u], axis=1)
    y = up[:, 0:l] * w[0]
    for j in range(1, width):
        y = y + up[:, j:j + l] * w[j]
    return y, up[:, up.shape[1] - (width - 1):]


def gated_delta_rule(q, k, v, g, beta, s0):
    b, l, h, dk = q.shape
    dv = v.shape[-1]
    c = min(GDN_CHUNK, l)
    n = -(-l // c)
    pad = n * c - l

    def blocks(t):
        t = jnp.pad(t, [(0, 0), (0, pad)] + [(0, 0)] * (t.ndim - 2))
        t = t.reshape((b, n, c) + t.shape[2:])
        return jnp.moveaxis(t, (1, 2), (0, 3))

    qc, kc, vc, gc, bc = blocks(q), blocks(k), blocks(v), blocks(g), blocks(beta)
    G = jnp.cumsum(gc, axis=-1)
    idx = jnp.arange(c)
    incl = idx[:, None] >= idx[None, :]
    strict = idx[:, None] > idx[None, :]
    diff = G[..., :, None] - G[..., None, :]
    decay = jnp.where(incl, jnp.exp(jnp.where(incl, diff, 0.0)), 0.0)
    kk = jnp.einsum('nbhid,nbhjd->nbhij', kc, kc)
    lower = jnp.where(strict, bc[..., :, None] * kk * decay, 0.0)
    eG = jnp.exp(G)
    rhs = jnp.concatenate([bc[..., None] * vc, (bc * eG)[..., None] * kc], axis=-1)
    sol = lax.linalg.triangular_solve(lower + jnp.eye(c, dtype=lower.dtype), rhs, left_side=True, lower=True, unit_diagonal=True)
    u_bar, w = sol[..., :dv], sol[..., dv:]
    qk = jnp.einsum('nbhid,nbhjd->nbhij', qc, kc) * decay
    q_dec = eG[..., None] * qc
    k_dec = jnp.exp(G[..., -1:] - G)[..., None] * kc
    g_tot = jnp.exp(G[..., -1])[..., None, None]

    def step(s, xs):
        u_bar_i, w_i, qk_i, q_i, k_i, gt_i = xs
        u = u_bar_i - jnp.einsum('bhck,bhkv->bhcv', w_i, s)
        o = jnp.einsum('bhck,bhkv->bhcv', q_i, s) + jnp.einsum('bhij,bhjv->bhiv', qk_i, u)
        s = gt_i * s + jnp.einsum('bhck,bhcv->bhkv', k_i, u)
        return s, o

    s, o = lax.scan(step, s0, (u_bar, w, qk, q_dec, k_dec, g_tot))
    o = jnp.moveaxis(o, (0, 3), (1, 2)).reshape(b, n * c, h, dv)[:, :l]
    return o, s


def gdn_branch(qkv_raw, z, b_raw, a_raw, conv_buf, s0, conv_w, a_log, dt_bias, norm_w):
    bsz, l, _ = qkv_raw.shape
    qkv, conv_buf_new = causal_conv(qkv_raw, conv_buf, conv_w)
    qkv = jax.nn.silu(qkv).astype(jnp.float32)
    q, k, v = jnp.split(qkv, (GDN_QK, 2 * GDN_QK), axis=-1)
    q = l2_norm(q.reshape(bsz, l, GDN_HEADS, GDN_DK)) * (GDN_DK ** -0.5)
    k = l2_norm(k.reshape(bsz, l, GDN_HEADS, GDN_DK))
    v = v.reshape(bsz, l, GDN_HEADS, GDN_DV)
    beta = jax.nn.sigmoid(b_raw.astype(jnp.float32))
    g = -jnp.exp(a_log.astype(jnp.float32)) * jax.nn.softplus(a_raw.astype(jnp.float32) + dt_bias.astype(jnp.float32))
    o, s = gated_delta_rule(q, k, v, g, beta, s0.astype(jnp.float32))
    o = rms_norm(o, norm_w) * jax.nn.silu(z.reshape(bsz, l, GDN_HEADS, GDN_DV).astype(jnp.float32))
    return o.reshape(bsz, l, GDN_V).astype(qkv_raw.dtype), conv_buf_new, s


def short_conv_branch(gate_b, gate_c, h, conv_buf, conv_w):
    u, conv_buf_new = causal_conv(gate_c * h, conv_buf, conv_w)
    return gate_b * u, conv_buf_new


def diff_weights(s1, s2, mask, lam):
    p1 = jax.nn.softmax(jnp.where(mask, s1, -jnp.inf), axis=-1)
    p2 = jax.nn.softmax(jnp.where(mask, s2, -jnp.inf), axis=-1)
    return p1 - lam * p2


def diff_head_norm(o, lam_init, norm_w):
    b, l = o.shape[0], o.shape[1]
    return (rms_norm(o, norm_w) * (1.0 - lam_init)).reshape(b, l, DIFF_V)


def diff_attention_prompt(q, k, v, lam, lam_init, norm_w):
    b, l, h, _ = q.shape
    qb_size = min(Q_BLOCK, l)
    nb = l // qb_size
    q_blocks = jnp.moveaxis(q.reshape(b, nb, qb_size, h, 2 * DIFF_HD), 1, 0)
    starts = jnp.arange(nb, dtype=jnp.int32) * qb_size
    k1, k2 = k[..., :DIFF_HD], k[..., DIFF_HD:]
    kpos = jnp.arange(l, dtype=jnp.int32)

    def block(args):
        qb, start = args
        s1 = jnp.einsum('bqhd,bkhd->bhqk', qb[..., :DIFF_HD], k1).astype(jnp.float32) * DIFF_SCALE
        s2 = jnp.einsum('bqhd,bkhd->bhqk', qb[..., DIFF_HD:], k2).astype(jnp.float32) * DIFF_SCALE
        mask = kpos[None, :] <= (start + jnp.arange(qb_size, dtype=jnp.int32))[:, None]
        p = diff_weights(s1, s2, mask, lam)
        return jnp.einsum('bhqk,bkhd->bqhd', p.astype(v.dtype), v)

    o = lax.map(block, (q_blocks, starts))
    o = jnp.moveaxis(o, 0, 1).reshape(b, l, h, DIFF_VD)
    return diff_head_norm(o, lam_init, norm_w)


def diff_attention_sample(q, k, v, k_past, v_past, lam, lam_init, norm_w):
    l = q.shape[1]
    past = k_past.shape[1]
    q1, q2 = q[..., :DIFF_HD], q[..., DIFF_HD:]
    s1 = jnp.concatenate([jnp.einsum('bqhd,bkhd->bhqk', q1, k_past[..., :DIFF_HD]),
                          jnp.einsum('bqhd,bkhd->bhqk', q1, k[..., :DIFF_HD])], axis=-1).astype(jnp.float32) * DIFF_SCALE
    s2 = jnp.concatenate([jnp.einsum('bqhd,bkhd->bhqk', q2, k_past[..., DIFF_HD:]),
                          jnp.einsum('bqhd,bkhd->bhqk', q2, k[..., DIFF_HD:])], axis=-1).astype(jnp.float32) * DIFF_SCALE
    mask = jnp.arange(past + l, dtype=jnp.int32)[None, :] <= (past + jnp.arange(l, dtype=jnp.int32))[:, None]
    p = diff_weights(s1, s2, mask, lam).astype(v.dtype)
    o = jnp.einsum('bhqk,bkhd->bqhd', p[..., :past], v_past) + jnp.einsum('bhqk,bkhd->bqhd', p[..., past:], v)
    return diff_head_norm(o, lam_init, norm_w)


def token_mixer(x, gdn_buf, gdn_s, sc_buf, attend, w_in, gdn_conv_w, gdn_a_log, gdn_dt_bias, gdn_norm_w, sc_conv_w, w_branch, w_o):
    b, l, _ = x.shape
    (a_qkv, a_z, a_b, a_a, s_b, s_c, s_h, d_q, d_k, d_v, gates) = jnp.split(x @ w_in, SPLIT_OFFSETS, axis=-1)
    y_a, gdn_buf_new, gdn_s_new = gdn_branch(a_qkv, a_z, a_b, a_a, gdn_buf, gdn_s, gdn_conv_w, gdn_a_log, gdn_dt_bias, gdn_norm_w)
    y_b, sc_buf_new = short_conv_branch(s_b, s_c, s_h, sc_buf, sc_conv_w)
    k_rows = d_k.reshape(b, l, DIFF_HEADS, 2 * DIFF_HD)
    v_rows = d_v.reshape(b, l, DIFF_HEADS, DIFF_VD)
    y_c = attend(d_q.reshape(b, l, DIFF_HEADS, 2 * DIFF_HD), k_rows, v_rows).astype(x.dtype)
    gates = jax.nn.sigmoid(gates)
    merged = (gates[..., :D_MODEL] * (y_a @ w_branch[0])
              + gates[..., D_MODEL:2 * D_MODEL] * (y_b @ w_branch[1])
              + gates[..., 2 * D_MODEL:] * (y_c @ w_branch[2]))
    return merged @ w_o, k_rows, v_rows, gdn_buf_new, gdn_s_new, sc_buf_new


def trunk_layer(x, gdn_buf, gdn_s, sc_buf, attend, w_in, gdn_conv_w, gdn_a_log, gdn_dt_bias, gdn_norm_w, sc_conv_w, w_branch, w_o, ln1_g, ln1_b, ln2_g, ln2_b, w_up, w_down):
    mix, k_rows, v_rows, gdn_buf_new, gdn_s_new, sc_buf_new = token_mixer(
        x, gdn_buf, gdn_s, sc_buf, attend, w_in, gdn_conv_w, gdn_a_log, gdn_dt_bias, gdn_norm_w, sc_conv_w, w_branch, w_o)
    x = layer_norm(DN_ALPHA * x + mix, ln1_g, ln1_b)
    mlp = jnp.square(jax.nn.relu(x @ w_up)) @ w_down
    x = layer_norm(DN_ALPHA * x + mlp, ln2_g, ln2_b)
    return x, k_rows, v_rows, gdn_s_new.astype(x.dtype), gdn_buf_new, sc_buf_new


def setup_inputs(seed: int = 0) -> dict:
    key = jax.random.key(seed)
    ks = jax.random.split(key, 26)
    f32 = jnp.float32
    n_pages = PAST_LEN // PAGE_SIZE
    n_used = DEC_BATCH * n_pages
    n_phys = n_used + n_used // 4
    page_table = jax.random.permutation(ks[0], n_phys)[:n_used].reshape(DEC_BATCH, n_pages).astype(jnp.int32)
    dt = jnp.exp(jax.random.uniform(ks[1], (DEPTH, GDN_HEADS), f32, math.log(1e-3), math.log(1e-1)))
    return {
        'x_prompt': jax.random.normal(ks[2], (BATCH, SEQ, D_MODEL), f32),
        'x_sample': jax.random.normal(ks[3], (DEC_BATCH, DEC_SEQ, D_MODEL), f32),
        'cache_k': jax.random.normal(ks[4], (n_phys, DEPTH, PAGE_SIZE, DIFF_HEADS, 2 * DIFF_HD), f32),
        'cache_v': jax.random.normal(ks[5], (n_phys, DEPTH, PAGE_SIZE, DIFF_HEADS, DIFF_VD), f32),
        'page_table': page_table,
        'state_gdn': 0.1 * jax.random.normal(ks[6], (DEC_BATCH, DEPTH, GDN_HEADS, GDN_DK, GDN_DV), f32),
        'state_gdn_conv': jax.random.normal(ks[7], (DEC_BATCH, DEPTH, GDN_CONV - 1, GDN_CONV_CH), f32),
        'state_sc_conv': jax.random.normal(ks[8], (DEC_BATCH, DEPTH, SC_CONV - 1, SC_WIDTH), f32),
        'w_in': jax.random.normal(ks[9], (DEPTH, D_MODEL, IN_COLS), f32) * D_MODEL ** -0.5,
        'gdn_conv_w': jax.random.normal(ks[10], (DEPTH, GDN_CONV, GDN_CONV_CH), f32) * GDN_CONV ** -0.5,
        'gdn_a_log': jnp.log(jax.random.uniform(ks[11], (DEPTH, GDN_HEADS), f32, 1.0, 16.0)),
        'gdn_dt_bias': dt + jnp.log(-jnp.expm1(-dt)),
        'gdn_norm_w': 1.0 + 0.02 * jax.random.normal(ks[12], (DEPTH, GDN_DV), f32),
        'sc_conv_w': jax.random.normal(ks[13], (DEPTH, SC_CONV, SC_WIDTH), f32) * SC_CONV ** -0.5,
        'diff_lambda': 0.1 * jax.random.normal(ks[14], (DEPTH, 4, DIFF_HD), f32),
        'diff_norm_w': 1.0 + 0.02 * jax.random.normal(ks[15], (DEPTH, DIFF_VD), f32),
        'w_branch': jax.random.normal(ks[16], (DEPTH, N_BRANCH, BRANCH_W, D_MODEL), f32) * (BRANCH_W ** -0.5 * DN_BETA),
        'w_o': jax.random.normal(ks[17], (DEPTH, D_MODEL, D_MODEL), f32) * (D_MODEL ** -0.5 * DN_BETA),
        'ln1_g': 1.0 + 0.02 * jax.random.normal(ks[18], (DEPTH, D_MODEL), f32),
        'ln1_b': 0.02 * jax.random.normal(ks[19], (DEPTH, D_MODEL), f32),
        'ln2_g': 1.0 + 0.02 * jax.random.normal(ks[20], (DEPTH, D_MODEL), f32),
        'ln2_b': 0.02 * jax.random.normal(ks[21], (DEPTH, D_MODEL), f32),
        'w_up': jax.random.normal(ks[22], (DEPTH, D_MODEL, D_FF), f32) * D_MODEL ** -0.5,
        'w_down': jax.random.normal(ks[23], (DEPTH, D_FF, D_MODEL), f32) * (D_FF ** -0.5 * DN_BETA),
    }


def reference(x_prompt, x_sample, cache_k, cache_v, page_table, state_gdn, state_gdn_conv, state_sc_conv,
              w_in, gdn_conv_w, gdn_a_log, gdn_dt_bias, gdn_norm_w, sc_conv_w, diff_lambda, diff_norm_w,
              w_branch, w_o, ln1_g, ln1_b, ln2_g, ln2_b, w_up, w_down):
    dtype = x_prompt.dtype
    b_p = x_prompt.shape[0]
    b_s = x_sample.shape[0]
    past = page_table.shape[1] * PAGE_SIZE
    xp, xs = x_prompt, x_sample
    rows_p = [[], [], [], [], []]
    rows_s = [[], [], [], [], []]
    for l in range(DEPTH):
        lam_init = 0.8 - 0.6 * math.exp(-0.3 * l)
        lq = diff_lambda[l].astype(jnp.float32)
        lam = jnp.exp(jnp.sum(lq[0] * lq[1])) - jnp.exp(jnp.sum(lq[2] * lq[3])) + lam_init
        weights_l = (w_in[l], gdn_conv_w[l], gdn_a_log[l], gdn_dt_bias[l], gdn_norm_w[l], sc_conv_w[l], w_branch[l], w_o[l],
                     ln1_g[l], ln1_b[l], ln2_g[l], ln2_b[l], w_up[l], w_down[l])
        attend_p = functools.partial(diff_attention_prompt, lam=lam, lam_init=lam_init, norm_w=diff_norm_w[l])
        out_p = trunk_layer(xp,
                            jnp.zeros((b_p, GDN_CONV - 1, GDN_CONV_CH), dtype),
                            jnp.zeros((b_p, GDN_HEADS, GDN_DK, GDN_DV), jnp.float32),
                            jnp.zeros((b_p, SC_CONV - 1, SC_WIDTH), dtype),
                            attend_p, *weights_l)
        xp = out_p[0]
        k_past = cache_k[page_table, l].reshape(b_s, past, DIFF_HEADS, 2 * DIFF_HD)
        v_past = cache_v[page_table, l].reshape(b_s, past, DIFF_HEADS, DIFF_VD)
        attend_s = functools.partial(diff_attention_sample, k_past=k_past, v_past=v_past, lam=lam, lam_init=lam_init, norm_w=diff_norm_w[l])
        out_s = trunk_layer(xs, state_gdn_conv[:, l], state_gdn[:, l], state_sc_conv[:, l], attend_s, *weights_l)
        xs = out_s[0]
        for i in range(5):
            rows_p[i].append(out_p[i + 1])
            rows_s[i].append(out_s[i + 1])
    k_p, v_p, gdn_p, gdn_conv_p, sc_conv_p = [jnp.stack(r, axis=1) for r in rows_p]
    k_s, v_s, gdn_s, gdn_conv_s, sc_conv_s = [jnp.stack(r, axis=1) for r in rows_s]
    return (xp, xs, k_p, v_p, gdn_p, gdn_conv_p, sc_conv_p, k_s, v_s, gdn_s, gdn_conv_s, sc_conv_s)
```

```python
import functools
import math

import jax
import jax.numpy as jnp
from jax import lax
from jax.experimental import pallas as pl
from jax.experimental.pallas import tpu as pltpu

F32 = jnp.float32
BF16 = jnp.bfloat16

D_MODEL = 1024
GDN_HEADS = 4
GDN_DK = 128
GDN_DV = 128
GDN_CONV = 4
GDN_QK = GDN_HEADS * GDN_DK
GDN_V = GDN_HEADS * GDN_DV
GDN_CONV_CH = 2 * GDN_QK + GDN_V
GDN_CHUNK = 64
SC_WIDTH = 512
SC_CONV = 3
DIFF_HEADS = 4
DIFF_HD = 64
DIFF_VD = 128
DIFF_QK = DIFF_HEADS * 2 * DIFF_HD
DIFF_V = DIFF_HEADS * DIFF_VD
DIFF_SCALE = DIFF_HD ** -0.5
PAGE_SIZE = 128
N_BRANCH = 3
BRANCH_W = 512
D_FF = 4 * D_MODEL
LN_EPS = 1e-5
RMS_EPS = 1e-6
NEG_BIG = -1e30

LANE = 128
SUBLANE = 8

COL_GATES = 0
COL_QKV = COL_GATES + N_BRANCH * D_MODEL
COL_Z = COL_QKV + GDN_CONV_CH
COL_SB = COL_Z + GDN_V
COL_SC = COL_SB + SC_WIDTH
COL_SH = COL_SC + SC_WIDTH
COL_DQ = COL_SH + SC_WIDTH
COL_DK = COL_DQ + DIFF_QK
COL_DV = COL_DK + DIFF_QK
COL_AB = COL_DV + DIFF_V
H_COLS = COL_AB + LANE
IN_PROJ_TN = H_COLS // 5

VMEM_LIMIT = 56 * 1024 * 1024


def _nt(a, b):
    return lax.dot_general(a, b, (((1,), (1,)), ((), ())), preferred_element_type=F32)


def _tn(a, b):
    return lax.dot_general(a, b, (((0,), (0,)), ((), ())), preferred_element_type=F32)


def _nn(a, b):
    return jnp.dot(a, b, preferred_element_type=F32)


def _split3(x):
    x1 = x.astype(BF16)
    r1 = x - x1.astype(F32)
    x2 = r1.astype(BF16)
    r2 = r1 - x2.astype(F32)
    return x1, x2, r2.astype(BF16)


def _split2(x):
    x1 = x.astype(BF16)
    return x1, (x - x1.astype(F32)).astype(BF16)


def _silu(x):
    return x * jax.nn.sigmoid(x)


def _layer_norm(x, g, b):
    mu = jnp.mean(x, axis=-1, keepdims=True)
    xc = x - mu
    var = jnp.mean(xc * xc, axis=-1, keepdims=True)
    return xc * lax.rsqrt(var + LN_EPS) * g + b


def _rms_norm(x, w):
    return x * lax.rsqrt(jnp.mean(x * x, axis=-1, keepdims=True) + RMS_EPS) * w


def _in_proj_kernel(x_ref, w_ref, o_ref):
    o_ref[...] = _nn(x_ref[...].astype(BF16), w_ref[...])


def _in_proj(x2, w_bf16, tm):
    n = x2.shape[0]
    return pl.pallas_call(
        _in_proj_kernel,
        grid=(n // tm, H_COLS // IN_PROJ_TN),
        in_specs=[pl.BlockSpec((tm, D_MODEL), lambda i, j: (i, 0)),
                  pl.BlockSpec((D_MODEL, IN_PROJ_TN), lambda i, j: (0, j))],
        out_specs=pl.BlockSpec((tm, IN_PROJ_TN), lambda i, j: (i, j)),
        out_shape=jax.ShapeDtypeStruct((n, H_COLS), F32),
        compiler_params=pltpu.CompilerParams(
            dimension_semantics=("parallel", "arbitrary"), vmem_limit_bytes=VMEM_LIMIT),
        name="in_proj",
    )(x2, w_bf16)


def _recurrent_kernel(qkv_ref, z_ref, sb_ref, sc_ref, sh_ref, ab_ref, s0_ref, gbuf0_ref, sbuf0_ref,
                      convw_ref, alog_ref, dtb_ref, normw_ref, scw_ref,
                      ya_ref, yb_ref, sout_ref, gbuf_out_ref, sbuf_out_ref,
                      ext_ref, scext_ref, act_ref, state_ref, *, tile, chunk, valid):
    t = pl.program_id(1)
    nt = pl.num_programs(1)
    pad = SUBLANE

    @pl.when(t == 0)
    def _():
        state_ref[...] = s0_ref[...]
        ext_ref[0:pad, :] = jnp.zeros((pad, GDN_CONV_CH), F32)
        ext_ref[pad - (GDN_CONV - 1):pad, :] = gbuf0_ref[...]
        scext_ref[0:pad, :] = jnp.zeros((pad, SC_WIDTH), F32)
        scext_ref[pad - (SC_CONV - 1):pad, :] = sbuf0_ref[...]

    ext_ref[pad:pad + tile, :] = qkv_ref[...]
    scext_ref[pad:pad + tile, :] = sc_ref[...] * sh_ref[...]

    rb = min(tile, 64)
    convw = convw_ref[...]
    scw = scw_ref[...]
    for r in range(tile // rb):
        base = pad + r * rb
        acc = ext_ref[base - 3:base - 3 + rb, :] * convw[0:1, :]
        for j in range(1, GDN_CONV):
            acc = acc + ext_ref[base - 3 + j:base - 3 + j + rb, :] * convw[j:j + 1, :]
        act_ref[r * rb:(r + 1) * rb, :] = _silu(acc)
        u = scext_ref[base - 2:base - 2 + rb, :] * scw[0:1, :]
        for j in range(1, SC_CONV):
            u = u + scext_ref[base - 2 + j:base - 2 + j + rb, :] * scw[j:j + 1, :]
        yb_ref[r * rb:(r + 1) * rb, :] = sb_ref[r * rb:(r + 1) * rb, :] * u

    @pl.when(t == nt - 1)
    def _():
        gbuf_out_ref[...] = ext_ref[pad + valid - (GDN_CONV - 1):pad + valid, :]
        sbuf_out_ref[...] = scext_ref[pad + valid - (SC_CONV - 1):pad + valid, :]

    ext_ref[0:pad, :] = ext_ref[tile:tile + pad, :]
    scext_ref[0:pad, :] = scext_ref[tile:tile + pad, :]

    c = chunk
    ri = lax.broadcasted_iota(jnp.int32, (c, c), 0)
    ci = lax.broadcasted_iota(jnp.int32, (c, c), 1)
    incl = ri >= ci
    strict = ri > ci
    tri = jnp.where(incl, 1.0, 0.0).astype(BF16)
    sel = jnp.where(lax.broadcasted_iota(jnp.int32, (SUBLANE, LANE), 1)
                    == lax.broadcasted_iota(jnp.int32, (SUBLANE, LANE), 0) + GDN_HEADS, 1.0, 0.0).astype(BF16)
    neg_a = -jnp.exp(alog_ref[...])
    dtb = dtb_ref[...]
    normw = normw_ref[...]
    nlev = max(1, int(math.ceil(math.log2(c))))
    rowmask = None
    if valid < tile:
        rowmask = jnp.where(lax.broadcasted_iota(jnp.int32, (c, 1), 0) < valid, 1.0, 0.0)

    def chunk_body(ic, carry):
        r0 = pl.multiple_of(ic * c, c)
        abc = ab_ref[pl.ds(r0, c), :]
        beta_blk = jax.nn.sigmoid(abc)
        xg = abc + dtb
        softplus = jnp.maximum(xg, 0.0) + jnp.log1p(jnp.exp(-jnp.abs(xg)))
        g_blk = neg_a * softplus
        if rowmask is not None:
            beta_blk = beta_blk * rowmask
            g_blk = g_blk * rowmask
        g1, g2, g3 = _split3(g_blk)
        cum = _nn(tri, g1) + _nn(tri, g2) + _nn(tri, g3)
        c1, c2, c3 = _split3(cum)
        cum_rows = _nt(sel, c1) + _nt(sel, c2) + _nt(sel, c3)
        for h in range(GDN_HEADS):
            q = act_ref[pl.ds(r0, c), h * GDN_DK:(h + 1) * GDN_DK]
            k = act_ref[pl.ds(r0, c), GDN_QK + h * GDN_DK:GDN_QK + (h + 1) * GDN_DK]
            v = act_ref[pl.ds(r0, c), 2 * GDN_QK + h * GDN_DV:2 * GDN_QK + (h + 1) * GDN_DV]
            q = q * lax.rsqrt(jnp.sum(q * q, axis=-1, keepdims=True) + RMS_EPS) * (GDN_DK ** -0.5)
            k = k * lax.rsqrt(jnp.sum(k * k, axis=-1, keepdims=True) + RMS_EPS)
            beta = beta_blk[:, h:h + 1]
            gc = cum[:, GDN_HEADS + h:GDN_HEADS + h + 1]
            gr = cum_rows[h:h + 1, :]
            if rowmask is not None:
                k = k * rowmask
                v = v * rowmask
            dec = jnp.where(incl, jnp.exp(jnp.where(incl, gc - gr, 0.0)), 0.0)
            k1, k2 = _split2(k)
            kk = _nt(k1, k1) + _nt(k1, k2) + _nt(k2, k1)
            lower = jnp.where(strict, beta * kk * dec, 0.0)
            x = -lower
            am = x
            for _ in range(nlev - 1):
                xb = x.astype(BF16)
                x = _nn(xb, xb)
                am = am + x + _nn(am.astype(BF16), x.astype(BF16))
            eg = jnp.exp(gc)
            rhs = jnp.concatenate([beta * v, (beta * eg) * k], axis=-1)
            sol = rhs + _nn(am.astype(BF16), rhs.astype(BF16))
            u_bar = sol[:, :GDN_DV]
            w = sol[:, GDN_DV:]
            qb = q.astype(BF16)
            qk = _nt(qb, k1) * dec
            g_last = gc[c - 1:c, :]
            q_dec = (eg * q).astype(BF16)
            k_dec = (jnp.exp(g_last - gc) * k).astype(BF16)
            s = state_ref[h]
            sb = s.astype(BF16)
            u = u_bar - _nn(w.astype(BF16), sb)
            ub = u.astype(BF16)
            o = _nn(q_dec, sb) + _nn(qk.astype(BF16), ub)
            state_ref[h] = jnp.exp(g_last) * s + _tn(k_dec, ub)
            zz = z_ref[pl.ds(r0, c), h * GDN_DV:(h + 1) * GDN_DV]
            ya_ref[pl.ds(r0, c), h * GDN_DV:(h + 1) * GDN_DV] = _rms_norm(o, normw) * _silu(zz)
        return carry

    lax.fori_loop(0, tile // c, chunk_body, 0)

    @pl.when(t == nt - 1)
    def _():
        sout_ref[...] = state_ref[...]


def _recurrent(h3, s0, gbuf0, sbuf0, convw, alog_row, dtb_row, normw, scw, *, tile, chunk, valid):
    b, l, _ = h3.shape
    nt = l // tile

    def col(width, start):
        return pl.BlockSpec((None, tile, width), lambda i, t: (i, t, start // width))

    def per_b(shape):
        return pl.BlockSpec((None,) + shape, lambda i, t: (i,) + (0,) * len(shape))

    def whole(shape):
        return pl.BlockSpec(shape, lambda i, t: (0,) * len(shape))

    pad = SUBLANE
    kern = functools.partial(_recurrent_kernel, tile=tile, chunk=chunk, valid=valid)
    return pl.pallas_call(
        kern,
        grid=(b, nt),
        in_specs=[col(GDN_CONV_CH, COL_QKV), col(GDN_V, COL_Z), col(SC_WIDTH, COL_SB), col(SC_WIDTH, COL_SC),
                  col(SC_WIDTH, COL_SH), col(LANE, COL_AB),
                  per_b((GDN_HEADS, GDN_DK, GDN_DV)), per_b((GDN_CONV - 1, GDN_CONV_CH)),
                  per_b((SC_CONV - 1, SC_WIDTH)),
                  whole((GDN_CONV, GDN_CONV_CH)), whole((1, LANE)), whole((1, LANE)), whole((1, GDN_DV)),
                  whole((SC_CONV, SC_WIDTH))],
        out_specs=[pl.BlockSpec((None, tile, GDN_V), lambda i, t: (i, t, 0)),
                   pl.BlockSpec((None, tile, SC_WIDTH), lambda i, t: (i, t, 0)),
                   per_b((GDN_HEADS, GDN_DK, GDN_DV)), per_b((GDN_CONV - 1, GDN_CONV_CH)),
                   per_b((SC_CONV - 1, SC_WIDTH))],
        out_shape=[jax.ShapeDtypeStruct((b, l, GDN_V), F32), jax.ShapeDtypeStruct((b, l, SC_WIDTH), F32),
                   jax.ShapeDtypeStruct((b, GDN_HEADS, GDN_DK, GDN_DV), F32),
                   jax.ShapeDtypeStruct((b, GDN_CONV - 1, GDN_CONV_CH), F32),
                   jax.ShapeDtypeStruct((b, SC_CONV - 1, SC_WIDTH), F32)],
        scratch_shapes=[pltpu.VMEM((tile + pad, GDN_CONV_CH), F32), pltpu.VMEM((tile + pad, SC_WIDTH), F32),
                        pltpu.VMEM((tile, GDN_CONV_CH), F32), pltpu.VMEM((GDN_HEADS, GDN_DK, GDN_DV), F32)],
        compiler_params=pltpu.CompilerParams(
            dimension_semantics=("parallel", "arbitrary"), vmem_limit_bytes=VMEM_LIMIT),
        name="recurrent",
    )(h3, h3, h3, h3, h3, h3, s0, gbuf0, sbuf0, convw, alog_row, dtb_row, normw, scw)


def _diff_lambda(lam_ref, lam_init):
    lq = lam_ref[...]
    a = jnp.sum(lq[0:1, :] * lq[1:2, :], axis=-1, keepdims=True)
    b = jnp.sum(lq[2:3, :] * lq[3:4, :], axis=-1, keepdims=True)
    return jnp.exp(a) - jnp.exp(b) + lam_init


def _attn_prompt_kernel(q_ref, k_ref, v_ref, lam_ref, nw_ref, o_ref, *, seq, tq, lam_init):
    lam = _diff_lambda(lam_ref, lam_init)
    kb = k_ref[...].astype(BF16)
    vb = v_ref[...].astype(BF16)
    nw = nw_ref[...]
    lane = lax.broadcasted_iota(jnp.int32, (1, 2 * DIFF_HD), 1)
    for i in range(seq // tq):
        kv = (i + 1) * tq
        q = q_ref[i * tq:(i + 1) * tq, :]
        q1 = jnp.where(lane < DIFF_HD, q, 0.0).astype(BF16)
        q2 = jnp.where(lane >= DIFF_HD, q, 0.0).astype(BF16)
        kbi = kb[:kv]
        mask = (lax.broadcasted_iota(jnp.int32, (tq, kv), 1)
                <= lax.broadcasted_iota(jnp.int32, (tq, kv), 0) + i * tq)
        s1 = jnp.where(mask, _nt(q1, kbi) * DIFF_SCALE, NEG_BIG)
        s2 = jnp.where(mask, _nt(q2, kbi) * DIFF_SCALE, NEG_BIG)
        e1 = jnp.exp(s1 - jnp.max(s1, axis=-1, keepdims=True))
        e2 = jnp.exp(s2 - jnp.max(s2, axis=-1, keepdims=True))
        r1 = 1.0 / jnp.sum(e1, axis=-1, keepdims=True)
        r2 = lam / jnp.sum(e2, axis=-1, keepdims=True)
        p = e1 * r1 - e2 * r2
        o = _nn(p.astype(BF16), vb[:kv])
        o_ref[i * tq:(i + 1) * tq, :] = _rms_norm(o, nw) * (1.0 - lam_init)


def _attn_prompt(h3, lam_p, nw, lam_init, tq):
    b, l, _ = h3.shape

    def head_col(start):
        return pl.BlockSpec((None, l, LANE), lambda i, h: (i, 0, start // LANE + h))

    kern = functools.partial(_attn_prompt_kernel, seq=l, tq=tq, lam_init=lam_init)
    return pl.pallas_call(
        kern,
        grid=(b, DIFF_HEADS),
        in_specs=[head_col(COL_DQ), head_col(COL_DK), head_col(COL_DV),
                  pl.BlockSpec((4, DIFF_HD), lambda i, h: (0, 0)),
                  pl.BlockSpec((1, DIFF_VD), lambda i, h: (0, 0))],
        out_specs=pl.BlockSpec((None, l, DIFF_VD), lambda i, h: (i, 0, h)),
        out_shape=jax.ShapeDtypeStruct((b, l, DIFF_V), F32),
        compiler_params=pltpu.CompilerParams(
            dimension_semantics=("parallel", "arbitrary"), vmem_limit_bytes=VMEM_LIMIT),
        name="attn_prompt",
    )(h3, h3, h3, lam_p, nw)


def _attn_sample_kernel(pt_ref, q_ref, kn_ref, vn_ref, lam_ref, nw_ref, *rest, pages, lam_init):
    k_refs = rest[:pages]
    v_refs = rest[pages:2 * pages]
    o_ref = rest[2 * pages]
    m_ref, l_ref, acc_ref = rest[2 * pages + 1:]
    j = pl.program_id(1)
    nj = pl.num_programs(1)
    rows = DIFF_HEADS * 2 * 4
    rpp = PAGE_SIZE * DIFF_HEADS

    @pl.when(j == 0)
    def _():
        m_ref[...] = jnp.full((rows, 1), NEG_BIG, F32)
        l_ref[...] = jnp.zeros((rows, 1), F32)
        acc_ref[...] = jnp.zeros((rows, DIFF_VD), F32)

    qb = q_ref[...].astype(BF16)

    def update(s, pv):
        m_old = m_ref[...]
        m_new = jnp.maximum(m_old, jnp.max(s, axis=-1, keepdims=True))
        alpha = jnp.exp(m_old - m_new)
        p = jnp.exp(s - m_new)
        l_ref[...] = alpha * l_ref[...] + jnp.sum(p, axis=-1, keepdims=True)
        acc_ref[...] = alpha * acc_ref[...] + pv(p.astype(BF16))
        m_ref[...] = m_new

    width = pages * rpp
    row_head = lax.broadcasted_iota(jnp.int32, (rows, width), 0) >> 3
    col_head = lax.broadcasted_iota(jnp.int32, (rows, width), 1) & (DIFF_HEADS - 1)
    s = jnp.concatenate([_nt(qb, k_refs[i][...].astype(BF16)) for i in range(pages)], axis=1) * DIFF_SCALE
    s = jnp.where(row_head == col_head, s, NEG_BIG)

    def pv_past(pb):
        acc = _nn(pb[:, 0:rpp], v_refs[0][...].astype(BF16))
        for i in range(1, pages):
            acc = acc + _nn(pb[:, i * rpp:(i + 1) * rpp], v_refs[i][...].astype(BF16))
        return acc

    update(s, pv_past)

    @pl.when(j == nj - 1)
    def _():
        nk = kn_ref.shape[0]
        r = lax.broadcasted_iota(jnp.int32, (rows, nk), 0)
        cc = lax.broadcasted_iota(jnp.int32, (rows, nk), 1)
        ok = ((cc & (DIFF_HEADS - 1)) == (r >> 3)) & ((cc >> 2) <= (r & 3))
        sn = jnp.where(ok, _nt(qb, kn_ref[...].astype(BF16)) * DIFF_SCALE, NEG_BIG)
        vnb = vn_ref[...].astype(BF16)
        update(sn, lambda pb: _nn(pb, vnb))
        lam = _diff_lambda(lam_ref, lam_init)
        o = acc_ref[...] / l_ref[...]
        nw = nw_ref[...]
        for h in range(DIFF_HEADS):
            blk = o[h * 8:(h + 1) * 8, :]
            d = blk - lam * pltpu.roll(blk, 4, 0)
            o_ref[h] = _rms_norm(d, nw) * (1.0 - lam_init)


def _attn_sample(q_rows, k_new, v_new, lam_p, nw, cache_k4, cache_v4, page_table, layer, lam_init, pages):
    b = q_rows.shape[0]
    n_pages = page_table.shape[1]
    rows = q_rows.shape[1]
    rpp = PAGE_SIZE * DIFF_HEADS

    def page_spec(i):
        return pl.BlockSpec((None, None, rpp, DIFF_VD), lambda bi, j, pt: (pt[bi, j * pages + i], layer, 0, 0))

    def per_b(shape):
        return pl.BlockSpec((None,) + shape, lambda bi, j, pt: (bi,) + (0,) * len(shape))

    kern = functools.partial(_attn_sample_kernel, pages=pages, lam_init=lam_init)
    grid_spec = pltpu.PrefetchScalarGridSpec(
        num_scalar_prefetch=1,
        grid=(b, n_pages // pages),
        in_specs=[per_b((rows, 2 * DIFF_HD)), per_b(k_new.shape[1:]), per_b(v_new.shape[1:]),
                  pl.BlockSpec((4, DIFF_HD), lambda bi, j, pt: (0, 0)),
                  pl.BlockSpec((1, DIFF_VD), lambda bi, j, pt: (0, 0))]
                 + [page_spec(i) for i in range(pages)] + [page_spec(i) for i in range(pages)],
        out_specs=per_b((DIFF_HEADS, 8, DIFF_VD)),
        scratch_shapes=[pltpu.VMEM((rows, 1), F32), pltpu.VMEM((rows, 1), F32), pltpu.VMEM((rows, DIFF_VD), F32)],
    )
    return pl.pallas_call(
        kern,
        grid_spec=grid_spec,
        out_shape=jax.ShapeDtypeStruct((b, DIFF_HEADS, 8, DIFF_VD), F32),
        compiler_params=pltpu.CompilerParams(
            dimension_semantics=("parallel", "arbitrary"), vmem_limit_bytes=VMEM_LIMIT),
        name="attn_sample",
    )(page_table, q_rows, k_new, v_new, lam_p, nw, *([cache_k4] * pages), *([cache_v4] * pages))


def _merge_kernel(x_ref, g_ref, ya_ref, yb_ref, yc_ref, wb_ref, wo_ref, lg_ref, lb_ref, o_ref, *, alpha):
    merged = None
    for i, y_ref in enumerate((ya_ref, yb_ref, yc_ref)):
        gate = jax.nn.sigmoid(g_ref[:, i * D_MODEL:(i + 1) * D_MODEL])
        term = gate * _nn(y_ref[...].astype(BF16), wb_ref[i])
        merged = term if merged is None else merged + term
    mix = _nn(merged.astype(BF16), wo_ref[...])
    o_ref[...] = _layer_norm(alpha * x_ref[...] + mix, lg_ref[...], lb_ref[...])


def _merge(x2, h2, ya2, yb2, yc2, wb, wo, lg, lb, tm, alpha):
    n = x2.shape[0]
    row = lambda w: pl.BlockSpec((tm, w), lambda i: (i, 0))
    return pl.pallas_call(
        functools.partial(_merge_kernel, alpha=alpha),
        grid=(n // tm,),
        in_specs=[row(D_MODEL), row(N_BRANCH * D_MODEL), row(BRANCH_W), row(BRANCH_W), row(BRANCH_W),
                  pl.BlockSpec((N_BRANCH, BRANCH_W, D_MODEL), lambda i: (0, 0, 0)),
                  pl.BlockSpec((D_MODEL, D_MODEL), lambda i: (0, 0)),
                  pl.BlockSpec((1, D_MODEL), lambda i: (0, 0)), pl.BlockSpec((1, D_MODEL), lambda i: (0, 0))],
        out_specs=row(D_MODEL),
        out_shape=jax.ShapeDtypeStruct((n, D_MODEL), F32),
        compiler_params=pltpu.CompilerParams(dimension_semantics=("parallel",), vmem_limit_bytes=VMEM_LIMIT),
        name="merge",
    )(x2, h2, ya2, yb2, yc2, wb, wo, lg, lb)


def _mlp_kernel(x_ref, wu_ref, wd_ref, lg_ref, lb_ref, o_ref, *, alpha, ff_chunk):
    x = x_ref[...]
    xb = x.astype(BF16)
    acc = None
    for cidx in range(D_FF // ff_chunk):
        up = jnp.maximum(_nn(xb, wu_ref[:, cidx * ff_chunk:(cidx + 1) * ff_chunk]), 0.0)
        term = _nn((up * up).astype(BF16), wd_ref[cidx * ff_chunk:(cidx + 1) * ff_chunk, :])
        acc = term if acc is None else acc + term
    o_ref[...] = _layer_norm(alpha * x + acc, lg_ref[...], lb_ref[...])


def _mlp(x2, wu, wd, lg, lb, tm, alpha):
    n = x2.shape[0]
    return pl.pallas_call(
        functools.partial(_mlp_kernel, alpha=alpha, ff_chunk=1024),
        grid=(n // tm,),
        in_specs=[pl.BlockSpec((tm, D_MODEL), lambda i: (i, 0)),
                  pl.BlockSpec((D_MODEL, D_FF), lambda i: (0, 0)),
                  pl.BlockSpec((D_FF, D_MODEL), lambda i: (0, 0)),
                  pl.BlockSpec((1, D_MODEL), lambda i: (0, 0)), pl.BlockSpec((1, D_MODEL), lambda i: (0, 0))],
        out_specs=pl.BlockSpec((tm, D_MODEL), lambda i: (i, 0)),
        out_shape=jax.ShapeDtypeStruct((n, D_MODEL), F32),
        compiler_params=pltpu.CompilerParams(dimension_semantics=("parallel",), vmem_limit_bytes=VMEM_LIMIT),
        name="mlp",
    )(x2, wu, wd, lg, lb)


def _regroup_w_in(w):
    o_z = GDN_CONV_CH
    o_ab = o_z + GDN_V
    o_sb = o_ab + 2 * GDN_HEADS
    o_gates = o_sb + 3 * SC_WIDTH + 2 * DIFF_QK + DIFF_V
    zeros = jnp.zeros((w.shape[0], LANE - 2 * GDN_HEADS), w.dtype)
    return jnp.concatenate([w[:, o_gates:], w[:, :o_ab], w[:, o_sb:o_gates], w[:, o_ab:o_sb], zeros],
                           axis=1).astype(BF16)


def _lane_row(vals, offset):
    n = vals.shape[0]
    return jnp.pad(vals.astype(F32), (offset, LANE - offset - n)).reshape(1, LANE)


def _layer(x3, gbuf0, s0, sbuf0, wl, lam_init, alpha, *, tile, chunk, valid, tm_proj, tm_tok, attend):
    b, l, _ = x3.shape
    x2 = x3.reshape(b * l, D_MODEL)
    h2 = _in_proj(x2, wl["w_in"], tm_proj)
    h3 = h2.reshape(b, l, H_COLS)
    ya, yb, s_new, gbuf_new, sbuf_new = _recurrent(
        h3, s0, gbuf0, sbuf0, wl["gdn_conv_w"], wl["alog_row"], wl["dtb_row"], wl["gdn_norm_w"], wl["sc_conv_w"],
        tile=tile, chunk=chunk, valid=valid)
    yc = attend(h3)
    x1 = _merge(x2, h2, ya.reshape(b * l, GDN_V), yb.reshape(b * l, SC_WIDTH), yc.reshape(b * l, DIFF_V),
                wl["w_branch"], wl["w_o"], wl["ln1_g"], wl["ln1_b"], tm_tok, alpha)
    x_out = _mlp(x1, wl["w_up"], wl["w_down"], wl["ln2_g"], wl["ln2_b"], tm_tok, alpha)
    return x_out.reshape(b, l, D_MODEL), h3, s_new, gbuf_new, sbuf_new


def kernel(x_prompt, x_sample, cache_k, cache_v, page_table, state_gdn, state_gdn_conv, state_sc_conv, w_in,
           gdn_conv_w, gdn_a_log, gdn_dt_bias, gdn_norm_w, sc_conv_w, diff_lambda, diff_norm_w, w_branch, w_o,
           ln1_g, ln1_b, ln2_g, ln2_b, w_up, w_down):
    depth = w_in.shape[0]
    alpha = (2 * depth) ** 0.25
    b_p, seq, _ = x_prompt.shape
    b_s, dec_seq, _ = x_sample.shape
    n_phys = cache_k.shape[0]
    assert dec_seq == 4 and seq % 256 == 0
    dec_pad = SUBLANE
    rpp = PAGE_SIZE * DIFF_HEADS
    cache_k4 = cache_k.reshape(n_phys, depth, rpp, 2 * DIFF_HD)
    cache_v4 = cache_v.reshape(n_phys, depth, rpp, DIFF_VD)

    xp = x_prompt
    xs = jnp.pad(x_sample, ((0, 0), (0, dec_pad - dec_seq), (0, 0)))
    half = (jnp.arange(2 * DIFF_HD) < DIFF_HD)
    map_mask = jnp.stack([half, ~half]).astype(F32)
    rows_p = [[], [], [], [], []]
    rows_s = [[], [], [], [], []]
    for l in range(depth):
        lam_init = 0.8 - 0.6 * math.exp(-0.3 * l)
        wl = {
            "w_in": _regroup_w_in(w_in[l]),
            "gdn_conv_w": gdn_conv_w[l], "sc_conv_w": sc_conv_w[l],
            "alog_row": _lane_row(gdn_a_log[l], GDN_HEADS), "dtb_row": _lane_row(gdn_dt_bias[l], GDN_HEADS),
            "gdn_norm_w": gdn_norm_w[l].reshape(1, GDN_DV),
            "w_branch": w_branch[l].astype(BF16), "w_o": w_o[l].astype(BF16),
            "ln1_g": ln1_g[l].reshape(1, D_MODEL), "ln1_b": ln1_b[l].reshape(1, D_MODEL),
            "ln2_g": ln2_g[l].reshape(1, D_MODEL), "ln2_b": ln2_b[l].reshape(1, D_MODEL),
            "w_up": w_up[l].astype(BF16), "w_down": w_down[l].astype(BF16),
        }
        lam_p = diff_lambda[l].astype(F32)
        nw = diff_norm_w[l].reshape(1, DIFF_VD)

        attend_p = functools.partial(_attn_prompt, lam_p=lam_p, nw=nw, lam_init=lam_init, tq=256)
        xp, h3p, s_p, gbuf_p, sbuf_p = _layer(
            xp, jnp.zeros((b_p, GDN_CONV - 1, GDN_CONV_CH), F32), jnp.zeros((b_p, GDN_HEADS, GDN_DK, GDN_DV), F32),
            jnp.zeros((b_p, SC_CONV - 1, SC_WIDTH), F32), wl, lam_init, alpha,
            tile=256, chunk=GDN_CHUNK, valid=256, tm_proj=1024, tm_tok=256, attend=attend_p)
        rows_p[0].append(h3p[:, :, COL_DK:COL_DK + DIFF_QK].reshape(b_p, seq, DIFF_HEADS, 2 * DIFF_HD))
        rows_p[1].append(h3p[:, :, COL_DV:COL_DV + DIFF_V].reshape(b_p, seq, DIFF_HEADS, DIFF_VD))
        rows_p[2].append(s_p)
        rows_p[3].append(gbuf_p)
        rows_p[4].append(sbuf_p)

        def attend_s(h3s):
            q = h3s[:, :dec_seq, COL_DQ:COL_DQ + DIFF_QK].reshape(b_s, dec_seq, DIFF_HEADS, 2 * DIFF_HD)
            q = jnp.transpose(q, (0, 2, 1, 3))[:, :, None] * map_mask[None, None, :, None, :]
            q_rows = q.reshape(b_s, DIFF_HEADS * 2 * dec_seq, 2 * DIFF_HD)
            k_new = h3s[:, :dec_seq, COL_DK:COL_DK + DIFF_QK].reshape(b_s, dec_seq * DIFF_HEADS, 2 * DIFF_HD)
            v_new = h3s[:, :dec_seq, COL_DV:COL_DV + DIFF_V].reshape(b_s, dec_seq * DIFF_HEADS, DIFF_VD)
            k_new = jnp.pad(k_new, ((0, 0), (0, LANE - dec_seq * DIFF_HEADS), (0, 0)))
            v_new = jnp.pad(v_new, ((0, 0), (0, LANE - dec_seq * DIFF_HEADS), (0, 0)))
            o = _attn_sample(q_rows, k_new, v_new, lam_p, nw, cache_k4, cache_v4, page_table, l, lam_init, pages=8)
            o = jnp.transpose(o[:, :, :dec_seq, :], (0, 2, 1, 3)).reshape(b_s, dec_seq, DIFF_V)
            return jnp.pad(o, ((0, 0), (0, dec_pad - dec_seq), (0, 0)))

        xs, h3s, s_s, gbuf_s, sbuf_s = _layer(
            xs, state_gdn_conv[:, l], state_gdn[:, l], state_sc_conv[:, l], wl, lam_init, alpha,
            tile=dec_pad, chunk=dec_pad, valid=dec_seq, tm_proj=b_s * dec_pad, tm_tok=b_s * dec_pad, attend=attend_s)
        rows_s[0].append(h3s[:, :dec_seq, COL_DK:COL_DK + DIFF_QK].reshape(b_s, dec_seq, DIFF_HEADS, 2 * DIFF_HD))
        rows_s[1].append(h3s[:, :dec_seq, COL_DV:COL_DV + DIFF_V].reshape(b_s, dec_seq, DIFF_HEADS, DIFF_VD))
        rows_s[2].append(s_s)
        rows_s[3].append(gbuf_s)
        rows_s[4].append(sbuf_s)

    outs_p = [jnp.stack(r, axis=1) for r in rows_p]
    outs_s = [jnp.stack(r, axis=1) for r in rows_s]
    return (xp, xs[:, :dec_seq], *outs_p, *outs_s)
```

```python
import functools
import math

import jax
import jax.numpy as jnp
from jax import lax
from jax.experimental import pallas as pl
from jax.experimental.pallas import tpu as pltpu

F32 = jnp.float32
BF16 = jnp.bfloat16

D_MODEL = 1024
GDN_HEADS = 4
GDN_DK = 128
GDN_DV = 128
GDN_CONV = 4
GDN_QK = GDN_HEADS * GDN_DK
GDN_V = GDN_HEADS * GDN_DV
GDN_CONV_CH = 2 * GDN_QK + GDN_V
GDN_CHUNK = 64
SC_WIDTH = 512
SC_CONV = 3
DIFF_HEADS = 4
DIFF_HD = 64
DIFF_VD = 128
DIFF_QK = DIFF_HEADS * 2 * DIFF_HD
DIFF_V = DIFF_HEADS * DIFF_VD
DIFF_SCALE = DIFF_HD ** -0.5
PAGE_SIZE = 128
N_BRANCH = 3
BRANCH_W = 512
D_FF = 4 * D_MODEL
LN_EPS = 1e-5
RMS_EPS = 1e-6
NEG_BIG = -1e30

LANE = 128
SUBLANE = 8

COL_QKV = 0
COL_Z = COL_QKV + GDN_CONV_CH
COL_SB = COL_Z + GDN_V
COL_SC = COL_SB + SC_WIDTH
COL_SH = COL_SC + SC_WIDTH
COL_AB = COL_SH + SC_WIDTH
REC_COLS = COL_AB + LANE
COL_DQ = REC_COLS
COL_DK = COL_DQ + DIFF_QK
COL_DV = COL_DK + DIFF_QK
ATT_COLS = DIFF_QK + DIFF_QK + DIFF_V
H_COLS = REC_COLS + ATT_COLS

VMEM_LIMIT = 56 * 1024 * 1024


def _nt(a, b):
    return lax.dot_general(a, b, (((1,), (1,)), ((), ())), preferred_element_type=F32)


def _tn(a, b):
    return lax.dot_general(a, b, (((0,), (0,)), ((), ())), preferred_element_type=F32)


def _nn(a, b):
    return jnp.dot(a, b, preferred_element_type=F32)


def _split3(x):
    x1 = x.astype(BF16)
    r1 = x - x1.astype(F32)
    x2 = r1.astype(BF16)
    r2 = r1 - x2.astype(F32)
    return x1, x2, r2.astype(BF16)


def _split2(x):
    x1 = x.astype(BF16)
    return x1, (x - x1.astype(F32)).astype(BF16)


def _silu(x):
    return x * jax.nn.sigmoid(x)


def _layer_norm(x, g, b):
    mu = jnp.mean(x, axis=-1, keepdims=True)
    xc = x - mu
    var = jnp.mean(xc * xc, axis=-1, keepdims=True)
    return xc * lax.rsqrt(var + LN_EPS) * g + b


def _rms_norm(x, w):
    return x * lax.rsqrt(jnp.mean(x * x, axis=-1, keepdims=True) + RMS_EPS) * w


def _const_spec(shape):
    return pl.BlockSpec(shape, lambda *_: (0,) * len(shape), pipeline_mode=pl.Buffered(1))


def _in_proj_kernel(x_ref, w_ref, o_ref):
    o_ref[...] = _nn(x_ref[...].astype(BF16), w_ref[...])


def _in_proj(x2, w_bf16):
    n, cols = x2.shape[0], w_bf16.shape[1]
    return pl.pallas_call(
        _in_proj_kernel,
        grid=(1,),
        in_specs=[pl.BlockSpec((n, D_MODEL), lambda i: (0, 0)), pl.BlockSpec((D_MODEL, cols), lambda i: (0, 0))],
        out_specs=pl.BlockSpec((n, cols), lambda i: (0, 0)),
        out_shape=jax.ShapeDtypeStruct((n, cols), F32),
        compiler_params=pltpu.CompilerParams(dimension_semantics=("arbitrary",), vmem_limit_bytes=VMEM_LIMIT),
        name="in_proj",
    )(x2, w_bf16)


def _recurrent_kernel(*refs, tile, chunk, valid, project):
    if project:
        x_ref, w_ref = refs[:2]
        refs = refs[2:]
    else:
        qkv_ref, z_ref, sb_ref, sc_ref, sh_ref, ab_ref = refs[:6]
        refs = refs[6:]
    (s0_ref, gbuf0_ref, sbuf0_ref, convw_ref, alog_ref, dtb_ref, normw_ref, scw_ref,
     ya_ref, yb_ref, sout_ref, gbuf_out_ref, sbuf_out_ref,
     ext_ref, scext_ref, act_ref, state_ref) = refs[:17]
    t = pl.program_id(1)
    nt = pl.num_programs(1)
    pad = SUBLANE

    @pl.when(t == 0)
    def _():
        state_ref[...] = s0_ref[...]
        ext_ref[0:pad, :] = jnp.zeros((pad, GDN_CONV_CH), F32)
        ext_ref[pad - (GDN_CONV - 1):pad, :] = gbuf0_ref[...]
        scext_ref[0:pad, :] = jnp.zeros((pad, SC_WIDTH), F32)
        scext_ref[pad - (SC_CONV - 1):pad, :] = sbuf0_ref[...]

    if project:
        z_ref, sb_ref, ab_ref = refs[17:20]
        xb = x_ref[...].astype(BF16)
        ext_ref[pad:pad + tile, :] = _nn(xb, w_ref[:, COL_QKV:COL_Z])
        z_ref[...] = _nn(xb, w_ref[:, COL_Z:COL_SB])
        sb_ref[...] = _nn(xb, w_ref[:, COL_SB:COL_SC])
        scext_ref[pad:pad + tile, :] = _nn(xb, w_ref[:, COL_SC:COL_SH]) * _nn(xb, w_ref[:, COL_SH:COL_AB])
        ab_ref[...] = _nn(xb, w_ref[:, COL_AB:REC_COLS])
    else:
        ext_ref[pad:pad + tile, :] = qkv_ref[...]
        scext_ref[pad:pad + tile, :] = sc_ref[...] * sh_ref[...]

    rb = min(tile, 64)
    convw = convw_ref[...]
    scw = scw_ref[...]
    for r in range(tile // rb):
        base = pad + r * rb
        acc = ext_ref[base - 3:base - 3 + rb, :] * convw[0:1, :]
        for j in range(1, GDN_CONV):
            acc = acc + ext_ref[base - 3 + j:base - 3 + j + rb, :] * convw[j:j + 1, :]
        act_ref[r * rb:(r + 1) * rb, :] = _silu(acc)
        u = scext_ref[base - 2:base - 2 + rb, :] * scw[0:1, :]
        for j in range(1, SC_CONV):
            u = u + scext_ref[base - 2 + j:base - 2 + j + rb, :] * scw[j:j + 1, :]
        yb_ref[r * rb:(r + 1) * rb, :] = sb_ref[r * rb:(r + 1) * rb, :] * u

    @pl.when(t == nt - 1)
    def _():
        gbuf_out_ref[...] = ext_ref[pad + valid - (GDN_CONV - 1):pad + valid, :]
        sbuf_out_ref[...] = scext_ref[pad + valid - (SC_CONV - 1):pad + valid, :]

    ext_ref[0:pad, :] = ext_ref[tile:tile + pad, :]
    scext_ref[0:pad, :] = scext_ref[tile:tile + pad, :]

    c = chunk
    nc = tile // c
    ri = lax.broadcasted_iota(jnp.int32, (c, c), 0)
    ci = lax.broadcasted_iota(jnp.int32, (c, c), 1)
    incl = ri >= ci
    strict = ri > ci
    tri = jnp.where(incl, 1.0, 0.0).astype(BF16)
    tri_t = jnp.where(ri <= ci, 1.0, 0.0).astype(BF16)
    neg_a = -jnp.exp(alog_ref[...])
    dtb = dtb_ref[...]
    normw = normw_ref[...]
    nlev = max(1, int(math.ceil(math.log2(c))))
    rowmask = None
    if valid < tile:
        rowmask = jnp.where(lax.broadcasted_iota(jnp.int32, (c, 1), 0) < valid, 1.0, 0.0)
    heads = range(GDN_HEADS)

    beta_blks, cums, cum_ts = [], [], []
    for ic in range(nc):
        abc = ab_ref[ic * c:(ic + 1) * c, :]
        beta_blk = jax.nn.sigmoid(abc)
        xg = abc + dtb
        g_blk = neg_a * (jnp.maximum(xg, 0.0) + jnp.log1p(jnp.exp(-jnp.abs(xg))))
        if rowmask is not None:
            beta_blk = beta_blk * rowmask
            g_blk = g_blk * rowmask
        g1, g2, g3 = _split3(g_blk)
        beta_blks.append(beta_blk)
        cums.append(_nn(tri, g1) + _nn(tri, g2) + _nn(tri, g3))
        cum_ts.append(_tn(g1, tri_t) + _tn(g2, tri_t) + _tn(g3, tri_t))

    pairs = [(ic, h) for ic in range(nc) for h in heads]
    npairs = len(pairs)
    qs, ks, vs, betas, gcs, decs, kks, qks = [], [], [], [], [], [], [], []
    for ic, h in pairs:
        rows = slice(ic * c, (ic + 1) * c)
        q = act_ref[rows, h * GDN_DK:(h + 1) * GDN_DK]
        k = act_ref[rows, GDN_QK + h * GDN_DK:GDN_QK + (h + 1) * GDN_DK]
        v = act_ref[rows, 2 * GDN_QK + h * GDN_DV:2 * GDN_QK + (h + 1) * GDN_DV]
        q = q * lax.rsqrt(jnp.sum(q * q, axis=-1, keepdims=True) + RMS_EPS) * (GDN_DK ** -0.5)
        k = k * lax.rsqrt(jnp.sum(k * k, axis=-1, keepdims=True) + RMS_EPS)
        if rowmask is not None:
            k = k * rowmask
            v = v * rowmask
        gc = cums[ic][:, GDN_HEADS + h:GDN_HEADS + h + 1]
        gr = cum_ts[ic][GDN_HEADS + h:GDN_HEADS + h + 1, :]
        k1, k2 = _split2(k)
        qs.append(q)
        ks.append(k)
        vs.append(v)
        betas.append(beta_blks[ic][:, h:h + 1])
        gcs.append(gc)
        decs.append(jnp.where(incl, jnp.exp(jnp.where(incl, gc - gr, 0.0)), 0.0))
        kks.append(_nt(k1, k1) + _nt(k1, k2) + _nt(k2, k1))
        qks.append(_nt(q.astype(BF16), k1))
    xs = [-jnp.where(strict, betas[p] * kks[p] * decs[p], 0.0) for p in range(npairs)]
    ams = list(xs)
    for _ in range(nlev - 1):
        xbs = [x.astype(BF16) for x in xs]
        xs = [_nn(xb, xb) for xb in xbs]
        ams = [am + x + _nn(am.astype(BF16), x.astype(BF16)) for am, x in zip(ams, xs)]
    egs = [jnp.exp(gc) for gc in gcs]
    rhss = [jnp.concatenate([betas[p] * vs[p], (betas[p] * egs[p]) * ks[p]], axis=-1) for p in range(npairs)]
    sols = [rhs + _nn(am.astype(BF16), rhs.astype(BF16)) for am, rhs in zip(ams, rhss)]
    u_bars = [sol[:, :GDN_DV] for sol in sols]
    wq = [jnp.concatenate([sols[p][:, GDN_DV:], egs[p] * qs[p]], axis=0).astype(BF16) for p in range(npairs)]
    qkb = [(qks[p] * decs[p]).astype(BF16) for p in range(npairs)]
    g_lasts = [gc[c - 1:c, :] for gc in gcs]
    k_decs = [(jnp.exp(g_lasts[p] - gcs[p]) * ks[p]).astype(BF16) for p in range(npairs)]

    states = [state_ref[h] for h in heads]
    for ic in range(nc):
        rows = slice(ic * c, (ic + 1) * c)
        sbs = [s.astype(BF16) for s in states]
        ws = [_nn(wq[ic * GDN_HEADS + h], sbs[h]) for h in heads]
        ubs = [(u_bars[ic * GDN_HEADS + h] - ws[h][:c]).astype(BF16) for h in heads]
        o2 = [_nn(qkb[ic * GDN_HEADS + h], ubs[h]) for h in heads]
        ds = [_tn(k_decs[ic * GDN_HEADS + h], ubs[h]) for h in heads]
        for h in heads:
            p = ic * GDN_HEADS + h
            states[h] = jnp.exp(g_lasts[p]) * states[h] + ds[h]
            o = ws[h][c:] + o2[h]
            zz = z_ref[rows, h * GDN_DV:(h + 1) * GDN_DV]
            ya_ref[rows, h * GDN_DV:(h + 1) * GDN_DV] = _rms_norm(o, normw) * _silu(zz)
    for h in heads:
        state_ref[h] = states[h]

    @pl.when(t == nt - 1)
    def _():
        sout_ref[...] = state_ref[...]


def _recurrent(src, w_rec, s0, gbuf0, sbuf0, convw, alog_row, dtb_row, normw, scw, *, tile, chunk, valid):
    b, l, _ = src.shape
    nt = l // tile
    project = w_rec is not None

    def col(width, start):
        return pl.BlockSpec((None, tile, width), lambda i, t: (i, t, start // width))

    def per_b(shape):
        return pl.BlockSpec((None,) + shape, lambda i, t: (i,) + (0,) * len(shape))

    if project:
        src_specs = [pl.BlockSpec((None, tile, D_MODEL), lambda i, t: (i, t, 0)), _const_spec((D_MODEL, REC_COLS))]
        src_args = [src, w_rec]
        extra_scratch = [pltpu.VMEM((tile, GDN_V), F32), pltpu.VMEM((tile, SC_WIDTH), F32),
                         pltpu.VMEM((tile, LANE), F32)]
    else:
        src_specs = [col(GDN_CONV_CH, COL_QKV), col(GDN_V, COL_Z), col(SC_WIDTH, COL_SB), col(SC_WIDTH, COL_SC),
                     col(SC_WIDTH, COL_SH), col(LANE, COL_AB)]
        src_args = [src] * 6
        extra_scratch = []
    pad = SUBLANE
    kern = functools.partial(_recurrent_kernel, tile=tile, chunk=chunk, valid=valid, project=project)
    return pl.pallas_call(
        kern,
        grid=(b, nt),
        in_specs=src_specs + [
            per_b((GDN_HEADS, GDN_DK, GDN_DV)), per_b((GDN_CONV - 1, GDN_CONV_CH)), per_b((SC_CONV - 1, SC_WIDTH)),
            _const_spec((GDN_CONV, GDN_CONV_CH)), _const_spec((1, LANE)), _const_spec((1, LANE)),
            _const_spec((1, GDN_DV)), _const_spec((SC_CONV, SC_WIDTH))],
        out_specs=[pl.BlockSpec((None, tile, GDN_V), lambda i, t: (i, t, 0)),
                   pl.BlockSpec((None, tile, SC_WIDTH), lambda i, t: (i, t, 0)),
                   per_b((GDN_HEADS, GDN_DK, GDN_DV)), per_b((GDN_CONV - 1, GDN_CONV_CH)),
                   per_b((SC_CONV - 1, SC_WIDTH))],
        out_shape=[jax.ShapeDtypeStruct((b, l, GDN_V), F32), jax.ShapeDtypeStruct((b, l, SC_WIDTH), F32),
                   jax.ShapeDtypeStruct((b, GDN_HEADS, GDN_DK, GDN_DV), F32),
                   jax.ShapeDtypeStruct((b, GDN_CONV - 1, GDN_CONV_CH), F32),
                   jax.ShapeDtypeStruct((b, SC_CONV - 1, SC_WIDTH), F32)],
        scratch_shapes=[pltpu.VMEM((tile + pad, GDN_CONV_CH), F32), pltpu.VMEM((tile + pad, SC_WIDTH), F32),
                        pltpu.VMEM((tile, GDN_CONV_CH), F32), pltpu.VMEM((GDN_HEADS, GDN_DK, GDN_DV), F32)]
                       + extra_scratch,
        compiler_params=pltpu.CompilerParams(
            dimension_semantics=("parallel", "arbitrary"), vmem_limit_bytes=VMEM_LIMIT),
        name="recurrent",
    )(*src_args, s0, gbuf0, sbuf0, convw, alog_row, dtb_row, normw, scw)


def _diff_lambda(lam_ref, lam_init):
    lq = lam_ref[...]
    a = jnp.sum(lq[0:1, :] * lq[1:2, :], axis=-1, keepdims=True)
    b = jnp.sum(lq[2:3, :] * lq[3:4, :], axis=-1, keepdims=True)
    return jnp.exp(a) - jnp.exp(b) + lam_init


def _attn_prompt_kernel(x_ref, w_ref, lam_ref, nw_ref, yc_ref, krow_ref, vrow_ref, q_s, k_s, v_s, o_s,
                        *, tq, nt, lam_init):
    t = pl.program_id(1)
    xb = x_ref[...].astype(BF16)
    q = _nn(xb, w_ref[:, 0:DIFF_QK])
    k = _nn(xb, w_ref[:, DIFF_QK:2 * DIFF_QK])
    v = _nn(xb, w_ref[:, 2 * DIFF_QK:ATT_COLS])
    krow_ref[...] = k
    vrow_ref[...] = v
    r0 = pl.multiple_of(t * tq, tq)
    lane = lax.broadcasted_iota(jnp.int32, (1, 2 * DIFF_HD), 1)
    for h in range(DIFF_HEADS):
        qh = q[:, h * LANE:(h + 1) * LANE]
        q_s[2 * h] = jnp.where(lane < DIFF_HD, qh, 0.0).astype(BF16)
        q_s[2 * h + 1] = jnp.where(lane >= DIFF_HD, qh, 0.0).astype(BF16)
        k_s[h, pl.ds(r0, tq), :] = k[:, h * LANE:(h + 1) * LANE].astype(BF16)
        v_s[h, pl.ds(r0, tq), :] = v[:, h * LANE:(h + 1) * LANE].astype(BF16)
    lam = _diff_lambda(lam_ref, lam_init)
    nw = nw_ref[...]

    for i in range(nt):
        kv = (i + 1) * tq

        @pl.when(t == i)
        def _(i=i, kv=kv):
            mask = (lax.broadcasted_iota(jnp.int32, (tq, kv), 1)
                    <= lax.broadcasted_iota(jnp.int32, (tq, kv), 0) + i * tq)

            def head_body(h, carry):
                kbi = k_s[h, 0:kv, :]
                s1 = jnp.where(mask, _nt(q_s[2 * h], kbi) * DIFF_SCALE, NEG_BIG)
                s2 = jnp.where(mask, _nt(q_s[2 * h + 1], kbi) * DIFF_SCALE, NEG_BIG)
                e1 = jnp.exp(s1 - jnp.max(s1, axis=-1, keepdims=True))
                e2 = jnp.exp(s2 - jnp.max(s2, axis=-1, keepdims=True))
                r1 = 1.0 / jnp.sum(e1, axis=-1, keepdims=True)
                r2 = lam / jnp.sum(e2, axis=-1, keepdims=True)
                p = e1 * r1 - e2 * r2
                o = _nn(p.astype(BF16), v_s[h, 0:kv, :])
                o_s[h] = _rms_norm(o, nw) * (1.0 - lam_init)
                return carry

            lax.fori_loop(0, DIFF_HEADS, head_body, 0)

    for h in range(DIFF_HEADS):
        yc_ref[:, h * DIFF_VD:(h + 1) * DIFF_VD] = o_s[h]


def _attn_prompt(x3, w_att, lam_p, nw, lam_init, tq):
    b, l, _ = x3.shape
    nt = l // tq
    kern = functools.partial(_attn_prompt_kernel, tq=tq, nt=nt, lam_init=lam_init)
    row = lambda w: pl.BlockSpec((None, tq, w), lambda i, t: (i, t, 0))
    return pl.pallas_call(
        kern,
        grid=(b, nt),
        in_specs=[row(D_MODEL), _const_spec((D_MODEL, ATT_COLS)), _const_spec((4, DIFF_HD)),
                  _const_spec((1, DIFF_VD))],
        out_specs=[row(DIFF_V), row(DIFF_QK), row(DIFF_V)],
        out_shape=[jax.ShapeDtypeStruct((b, l, DIFF_V), F32), jax.ShapeDtypeStruct((b, l, DIFF_QK), F32),
                   jax.ShapeDtypeStruct((b, l, DIFF_V), F32)],
        scratch_shapes=[pltpu.VMEM((2 * DIFF_HEADS, tq, 2 * DIFF_HD), BF16),
                        pltpu.VMEM((DIFF_HEADS, l, 2 * DIFF_HD), BF16), pltpu.VMEM((DIFF_HEADS, l, DIFF_VD), BF16),
                        pltpu.VMEM((DIFF_HEADS, tq, DIFF_VD), F32)],
        compiler_params=pltpu.CompilerParams(
            dimension_semantics=("parallel", "arbitrary"), vmem_limit_bytes=VMEM_LIMIT),
        name="attn_prompt",
    )(x3, w_att, lam_p, nw)


def _attn_sample_kernel(pt_ref, q_ref, kn_ref, vn_ref, lam_ref, nw_ref, *rest, pages, lam_init):
    k_refs = rest[:pages]
    v_refs = rest[pages:2 * pages]
    o_ref = rest[2 * pages]
    m_ref, l_ref, acc_ref = rest[2 * pages + 1:]
    j = pl.program_id(1)
    nj = pl.num_programs(1)
    rows = DIFF_HEADS * 2 * 4
    rpp = PAGE_SIZE * DIFF_HEADS

    @pl.when(j == 0)
    def _():
        m_ref[...] = jnp.full((rows, 1), NEG_BIG, F32)
        l_ref[...] = jnp.zeros((rows, 1), F32)
        acc_ref[...] = jnp.zeros((rows, DIFF_VD), F32)

    qb = q_ref[...].astype(BF16)

    def update(s, pv):
        m_old = m_ref[...]
        m_new = jnp.maximum(m_old, jnp.max(s, axis=-1, keepdims=True))
        alpha = jnp.exp(m_old - m_new)
        p = jnp.exp(s - m_new)
        l_ref[...] = alpha * l_ref[...] + jnp.sum(p, axis=-1, keepdims=True)
        acc_ref[...] = alpha * acc_ref[...] + pv(p.astype(BF16))
        m_ref[...] = m_new

    width = pages * rpp
    row_head = lax.broadcasted_iota(jnp.int32, (rows, width), 0) >> 3
    col_head = lax.broadcasted_iota(jnp.int32, (rows, width), 1) & (DIFF_HEADS - 1)
    s = jnp.concatenate([_nt(qb, k_refs[i][...].astype(BF16)) for i in range(pages)], axis=1) * DIFF_SCALE
    s = jnp.where(row_head == col_head, s, NEG_BIG)

    def pv_past(pb):
        acc = _nn(pb[:, 0:rpp], v_refs[0][...].astype(BF16))
        for i in range(1, pages):
            acc = acc + _nn(pb[:, i * rpp:(i + 1) * rpp], v_refs[i][...].astype(BF16))
        return acc

    update(s, pv_past)

    @pl.when(j == nj - 1)
    def _():
        nk = kn_ref.shape[0]
        r = lax.broadcasted_iota(jnp.int32, (rows, nk), 0)
        cc = lax.broadcasted_iota(jnp.int32, (rows, nk), 1)
        ok = ((cc & (DIFF_HEADS - 1)) == (r >> 3)) & ((cc >> 2) <= (r & 3))
        sn = jnp.where(ok, _nt(qb, kn_ref[...].astype(BF16)) * DIFF_SCALE, NEG_BIG)
        vnb = vn_ref[...].astype(BF16)
        update(sn, lambda pb: _nn(pb, vnb))
        lam = _diff_lambda(lam_ref, lam_init)
        o = acc_ref[...] / l_ref[...]
        nw = nw_ref[...]
        for h in range(DIFF_HEADS):
            blk = o[h * 8:(h + 1) * 8, :]
            d = blk - lam * pltpu.roll(blk, 4, 0)
            o_ref[h] = _rms_norm(d, nw) * (1.0 - lam_init)


def _attn_sample(q_rows, k_new, v_new, lam_p, nw, cache_k4, cache_v4, page_table, layer, lam_init, pages):
    b = q_rows.shape[0]
    n_pages = page_table.shape[1]
    rows = q_rows.shape[1]
    rpp = PAGE_SIZE * DIFF_HEADS

    def page_spec(i):
        return pl.BlockSpec((None, None, rpp, DIFF_VD), lambda bi, j, pt: (pt[bi, j * pages + i], layer, 0, 0))

    def per_b(shape):
        return pl.BlockSpec((None,) + shape, lambda bi, j, pt: (bi,) + (0,) * len(shape))

    kern = functools.partial(_attn_sample_kernel, pages=pages, lam_init=lam_init)
    grid_spec = pltpu.PrefetchScalarGridSpec(
        num_scalar_prefetch=1,
        grid=(b, n_pages // pages),
        in_specs=[per_b((rows, 2 * DIFF_HD)), per_b(k_new.shape[1:]), per_b(v_new.shape[1:]),
                  pl.BlockSpec((4, DIFF_HD), lambda bi, j, pt: (0, 0)),
                  pl.BlockSpec((1, DIFF_VD), lambda bi, j, pt: (0, 0))]
                 + [page_spec(i) for i in range(pages)] + [page_spec(i) for i in range(pages)],
        out_specs=per_b((DIFF_HEADS, 8, DIFF_VD)),
        scratch_shapes=[pltpu.VMEM((rows, 1), F32), pltpu.VMEM((rows, 1), F32), pltpu.VMEM((rows, DIFF_VD), F32)],
    )
    return pl.pallas_call(
        kern,
        grid_spec=grid_spec,
        out_shape=jax.ShapeDtypeStruct((b, DIFF_HEADS, 8, DIFF_VD), F32),
        compiler_params=pltpu.CompilerParams(
            dimension_semantics=("parallel", "arbitrary"), vmem_limit_bytes=VMEM_LIMIT),
        name="attn_sample",
    )(page_table, q_rows, k_new, v_new, lam_p, nw, *([cache_k4] * pages), *([cache_v4] * pages))


def _merge_mlp_kernel(x_ref, ya_ref, yb_ref, yc_ref, wg_ref, wb_ref, wo_ref, l1g_ref, l1b_ref,
                      wu_ref, wd_ref, l2g_ref, l2b_ref, o_ref, *, alpha, ff_chunk):
    x = x_ref[...]
    xb = x.astype(BF16)
    merged = None
    for i, y_ref in enumerate((ya_ref, yb_ref, yc_ref)):
        gate = jax.nn.sigmoid(_nn(xb, wg_ref[:, i * D_MODEL:(i + 1) * D_MODEL]))
        term = gate * _nn(y_ref[...].astype(BF16), wb_ref[i])
        merged = term if merged is None else merged + term
    mix = _nn(merged.astype(BF16), wo_ref[...])
    x1 = _layer_norm(alpha * x + mix, l1g_ref[...], l1b_ref[...])
    x1b = x1.astype(BF16)
    acc = None
    for cidx in range(D_FF // ff_chunk):
        up = jnp.maximum(_nn(x1b, wu_ref[:, cidx * ff_chunk:(cidx + 1) * ff_chunk]), 0.0)
        term = _nn((up * up).astype(BF16), wd_ref[cidx * ff_chunk:(cidx + 1) * ff_chunk, :])
        acc = term if acc is None else acc + term
    o_ref[...] = _layer_norm(alpha * x1 + acc, l2g_ref[...], l2b_ref[...])


def _merge_mlp(x2, ya2, yb2, yc2, wl, tm, alpha):
    n = x2.shape[0]
    row = lambda w: pl.BlockSpec((tm, w), lambda i: (i, 0))
    return pl.pallas_call(
        functools.partial(_merge_mlp_kernel, alpha=alpha, ff_chunk=1024),
        grid=(n // tm,),
        in_specs=[row(D_MODEL), row(BRANCH_W), row(BRANCH_W), row(BRANCH_W),
                  _const_spec((D_MODEL, N_BRANCH * D_MODEL)), _const_spec((N_BRANCH, BRANCH_W, D_MODEL)),
                  _const_spec((D_MODEL, D_MODEL)), _const_spec((1, D_MODEL)), _const_spec((1, D_MODEL)),
                  _const_spec((D_MODEL, D_FF)), _const_spec((D_FF, D_MODEL)),
                  _const_spec((1, D_MODEL)), _const_spec((1, D_MODEL))],
        out_specs=row(D_MODEL),
        out_shape=jax.ShapeDtypeStruct((n, D_MODEL), F32),
        compiler_params=pltpu.CompilerParams(dimension_semantics=("parallel",), vmem_limit_bytes=VMEM_LIMIT),
        name="merge_mlp",
    )(x2, ya2, yb2, yc2, wl["w_gates"], wl["w_branch"], wl["w_o"], wl["ln1_g"], wl["ln1_b"],
      wl["w_up"], wl["w_down"], wl["ln2_g"], wl["ln2_b"])


def _regroup_w_in(w):
    o_z = GDN_CONV_CH
    o_ab = o_z + GDN_V
    o_sb = o_ab + 2 * GDN_HEADS
    o_dq = o_sb + 3 * SC_WIDTH
    o_gates = o_dq + 2 * DIFF_QK + DIFF_V
    zeros = jnp.zeros((w.shape[0], LANE - 2 * GDN_HEADS), w.dtype)
    w_h = jnp.concatenate([w[:, :o_ab], w[:, o_sb:o_dq], w[:, o_ab:o_sb], zeros, w[:, o_dq:o_gates]], axis=1)
    return w_h.astype(BF16), w[:, o_gates:].astype(BF16)


def _lane_row(vals, offset):
    n = vals.shape[0]
    return jnp.pad(vals.astype(F32), (offset, LANE - offset - n)).reshape(1, LANE)


def kernel(x_prompt, x_sample, cache_k, cache_v, page_table, state_gdn, state_gdn_conv, state_sc_conv, w_in,
           gdn_conv_w, gdn_a_log, gdn_dt_bias, gdn_norm_w, sc_conv_w, diff_lambda, diff_norm_w, w_branch, w_o,
           ln1_g, ln1_b, ln2_g, ln2_b, w_up, w_down):
    depth = w_in.shape[0]
    alpha = (2 * depth) ** 0.25
    b_p, seq, _ = x_prompt.shape
    b_s, dec_seq, _ = x_sample.shape
    n_phys = cache_k.shape[0]
    assert dec_seq == 4 and seq % 256 == 0
    dec_pad = SUBLANE
    rpp = PAGE_SIZE * DIFF_HEADS
    cache_k4 = cache_k.reshape(n_phys, depth, rpp, 2 * DIFF_HD)
    cache_v4 = cache_v.reshape(n_phys, depth, rpp, DIFF_VD)
    prompt_tile = 256
    tok_tile = 256

    xp = x_prompt
    xs = jnp.pad(x_sample, ((0, 0), (0, dec_pad - dec_seq), (0, 0)))
    half = (jnp.arange(2 * DIFF_HD) < DIFF_HD)
    map_mask = jnp.stack([half, ~half]).astype(F32)
    rows_p = [[], [], [], [], []]
    rows_s = [[], [], [], [], []]
    for l in range(depth):
        lam_init = 0.8 - 0.6 * math.exp(-0.3 * l)
        w_h, w_gates = _regroup_w_in(w_in[l])
        wl = {
            "w_gates": w_gates,
            "w_branch": w_branch[l].astype(BF16), "w_o": w_o[l].astype(BF16),
            "ln1_g": ln1_g[l].reshape(1, D_MODEL), "ln1_b": ln1_b[l].reshape(1, D_MODEL),
            "ln2_g": ln2_g[l].reshape(1, D_MODEL), "ln2_b": ln2_b[l].reshape(1, D_MODEL),
            "w_up": w_up[l].astype(BF16), "w_down": w_down[l].astype(BF16),
        }
        rec_w = (gdn_conv_w[l], _lane_row(gdn_a_log[l], GDN_HEADS), _lane_row(gdn_dt_bias[l], GDN_HEADS),
                 gdn_norm_w[l].reshape(1, GDN_DV), sc_conv_w[l])
        lam_p = diff_lambda[l].astype(F32)
        nw = diff_norm_w[l].reshape(1, DIFF_VD)

        ya, yb, s_p, gbuf_p, sbuf_p = _recurrent(
            xp, w_h[:, :REC_COLS], jnp.zeros((b_p, GDN_HEADS, GDN_DK, GDN_DV), F32),
            jnp.zeros((b_p, GDN_CONV - 1, GDN_CONV_CH), F32), jnp.zeros((b_p, SC_CONV - 1, SC_WIDTH), F32),
            *rec_w, tile=prompt_tile, chunk=GDN_CHUNK, valid=prompt_tile)
        yc, k_rows, v_rows = _attn_prompt(xp, w_h[:, REC_COLS:], lam_p, nw, lam_init, tq=prompt_tile)
        n_p = b_p * seq
        xp = _merge_mlp(xp.reshape(n_p, D_MODEL), ya.reshape(n_p, GDN_V), yb.reshape(n_p, SC_WIDTH),
                        yc.reshape(n_p, DIFF_V), wl, tok_tile, alpha).reshape(b_p, seq, D_MODEL)
        rows_p[0].append(k_rows.reshape(b_p, seq, DIFF_HEADS, 2 * DIFF_HD))
        rows_p[1].append(v_rows.reshape(b_p, seq, DIFF_HEADS, DIFF_VD))
        rows_p[2].append(s_p)
        rows_p[3].append(gbuf_p)
        rows_p[4].append(sbuf_p)

        n_s = b_s * dec_pad
        h3s = _in_proj(xs.reshape(n_s, D_MODEL), w_h).reshape(b_s, dec_pad, H_COLS)
        ya, yb, s_s, gbuf_s, sbuf_s = _recurrent(
            h3s, None, state_gdn[:, l], state_gdn_conv[:, l], state_sc_conv[:, l],
            *rec_w, tile=dec_pad, chunk=dec_pad, valid=dec_seq)
        q = h3s[:, :dec_seq, COL_DQ:COL_DQ + DIFF_QK].reshape(b_s, dec_seq, DIFF_HEADS, 2 * DIFF_HD)
        q = jnp.transpose(q, (0, 2, 1, 3))[:, :, None] * map_mask[None, None, :, None, :]
        q_rows = q.reshape(b_s, DIFF_HEADS * 2 * dec_seq, 2 * DIFF_HD)
        k_new = h3s[:, :dec_seq, COL_DK:COL_DK + DIFF_QK].reshape(b_s, dec_seq * DIFF_HEADS, 2 * DIFF_HD)
        v_new = h3s[:, :dec_seq, COL_DV:COL_DV + DIFF_V].reshape(b_s, dec_seq * DIFF_HEADS, DIFF_VD)
        o = _attn_sample(q_rows, jnp.pad(k_new, ((0, 0), (0, LANE - dec_seq * DIFF_HEADS), (0, 0))),
                         jnp.pad(v_new, ((0, 0), (0, LANE - dec_seq * DIFF_HEADS), (0, 0))),
                         lam_p, nw, cache_k4, cache_v4, page_table, l, lam_init, pages=8)
        yc = jnp.transpose(o[:, :, :dec_seq, :], (0, 2, 1, 3)).reshape(b_s, dec_seq, DIFF_V)
        yc = jnp.pad(yc, ((0, 0), (0, dec_pad - dec_seq), (0, 0)))
        xs = _merge_mlp(xs.reshape(n_s, D_MODEL), ya.reshape(n_s, GDN_V), yb.reshape(n_s, SC_WIDTH),
                        yc.reshape(n_s, DIFF_V), wl, n_s, alpha).reshape(b_s, dec_pad, D_MODEL)
        rows_s[0].append(k_new.reshape(b_s, dec_seq, DIFF_HEADS, 2 * DIFF_HD))
        rows_s[1].append(v_new.reshape(b_s, dec_seq, DIFF_HEADS, DIFF_VD))
        rows_s[2].append(s_s)
        rows_s[3].append(gbuf_s)
        rows_s[4].append(sbuf_s)

    outs_p = [jnp.stack(r, axis=1) for r in rows_p]
    outs_s = [jnp.stack(r, axis=1) for r in rows_s]
    return (xp, xs[:, :dec_seq], *outs_p, *outs_s)
```

```python
import functools
import math

import jax
import jax.numpy as jnp
from jax import lax
from jax.experimental import pallas as pl
from jax.experimental.pallas import tpu as pltpu

F32 = jnp.float32
BF16 = jnp.bfloat16

D_MODEL = 1024
GDN_HEADS = 4
GDN_DK = 128
GDN_DV = 128
GDN_CONV = 4
GDN_QK = GDN_HEADS * GDN_DK
GDN_V = GDN_HEADS * GDN_DV
GDN_CONV_CH = 2 * GDN_QK + GDN_V
GDN_CHUNK = 64
SC_WIDTH = 512
SC_CONV = 3
DIFF_HEADS = 4
DIFF_HD = 64
DIFF_VD = 128
DIFF_QK = DIFF_HEADS * 2 * DIFF_HD
DIFF_V = DIFF_HEADS * DIFF_VD
DIFF_SCALE = DIFF_HD ** -0.5
PAGE_SIZE = 128
N_BRANCH = 3
BRANCH_W = 512
D_FF = 4 * D_MODEL
LN_EPS = 1e-5
RMS_EPS = 1e-6
NEG_BIG = -1e30

LANE = 128
SUBLANE = 8

COL_QKV = 0
COL_Z = COL_QKV + GDN_CONV_CH
COL_SB = COL_Z + GDN_V
COL_SC = COL_SB + SC_WIDTH
COL_SH = COL_SC + SC_WIDTH
COL_AB = COL_SH + SC_WIDTH
REC_COLS = COL_AB + LANE
COL_DQ = REC_COLS
COL_DK = COL_DQ + DIFF_QK
COL_DV = COL_DK + DIFF_QK
ATT_COLS = DIFF_QK + DIFF_QK + DIFF_V
H_COLS = REC_COLS + ATT_COLS

VMEM_LIMIT = 56 * 1024 * 1024


def _nt(a, b):
    return lax.dot_general(a, b, (((1,), (1,)), ((), ())), preferred_element_type=F32)


def _tn(a, b):
    return lax.dot_general(a, b, (((0,), (0,)), ((), ())), preferred_element_type=F32)


def _nn(a, b):
    return jnp.dot(a, b, preferred_element_type=F32)


def _split3(x):
    x1 = x.astype(BF16)
    r1 = x - x1.astype(F32)
    x2 = r1.astype(BF16)
    r2 = r1 - x2.astype(F32)
    return x1, x2, r2.astype(BF16)


def _split2(x):
    x1 = x.astype(BF16)
    return x1, (x - x1.astype(F32)).astype(BF16)


def _silu(x):
    return x * jax.nn.sigmoid(x)


def _layer_norm(x, g, b):
    mu = jnp.mean(x, axis=-1, keepdims=True)
    xc = x - mu
    var = jnp.mean(xc * xc, axis=-1, keepdims=True)
    return xc * lax.rsqrt(var + LN_EPS) * g + b


def _rms_norm(x, w):
    return x * lax.rsqrt(jnp.mean(x * x, axis=-1, keepdims=True) + RMS_EPS) * w


def _const_spec(shape):
    return pl.BlockSpec(shape, lambda *_: (0,) * len(shape), pipeline_mode=pl.Buffered(1))


def _in_proj_kernel(x_ref, w_ref, o_ref):
    o_ref[...] = _nn(x_ref[...].astype(BF16), w_ref[...])


def _in_proj(x2, w_bf16):
    n, cols = x2.shape[0], w_bf16.shape[1]
    return pl.pallas_call(
        _in_proj_kernel,
        grid=(1,),
        in_specs=[pl.BlockSpec((n, D_MODEL), lambda i: (0, 0)), pl.BlockSpec((D_MODEL, cols), lambda i: (0, 0))],
        out_specs=pl.BlockSpec((n, cols), lambda i: (0, 0)),
        out_shape=jax.ShapeDtypeStruct((n, cols), F32),
        compiler_params=pltpu.CompilerParams(dimension_semantics=("arbitrary",), vmem_limit_bytes=VMEM_LIMIT),
        name="in_proj",
    )(x2, w_bf16)


def _recurrent_kernel(*refs, tile, chunk, valid, project):
    if project:
        x_ref, w_ref = refs[:2]
        refs = refs[2:]
    else:
        qkv_ref, z_ref, sb_ref, sc_ref, sh_ref, ab_ref = refs[:6]
        refs = refs[6:]
    (s0_ref, gbuf0_ref, sbuf0_ref, convw_ref, alog_ref, dtb_ref, normw_ref, scw_ref,
     ya_ref, yb_ref, sout_ref, gbuf_out_ref, sbuf_out_ref,
     ext_ref, scext_ref, act_ref, state_ref) = refs[:17]
    t = pl.program_id(1)
    nt = pl.num_programs(1)
    pad = SUBLANE

    @pl.when(t == 0)
    def _():
        state_ref[...] = s0_ref[...]
        ext_ref[0:pad, :] = jnp.zeros((pad, GDN_CONV_CH), F32)
        ext_ref[pad - (GDN_CONV - 1):pad, :] = gbuf0_ref[...]
        scext_ref[0:pad, :] = jnp.zeros((pad, SC_WIDTH), F32)
        scext_ref[pad - (SC_CONV - 1):pad, :] = sbuf0_ref[...]

    if project:
        z_ref, sb_ref, ab_ref = refs[17:20]
        xb = x_ref[...].astype(BF16)
        ext_ref[pad:pad + tile, :] = _nn(xb, w_ref[:, COL_QKV:COL_Z])
        z_ref[...] = _nn(xb, w_ref[:, COL_Z:COL_SB])
        sb_ref[...] = _nn(xb, w_ref[:, COL_SB:COL_SC])
        scext_ref[pad:pad + tile, :] = _nn(xb, w_ref[:, COL_SC:COL_SH]) * _nn(xb, w_ref[:, COL_SH:COL_AB])
        ab_ref[...] = _nn(xb, w_ref[:, COL_AB:REC_COLS])
    else:
        ext_ref[pad:pad + tile, :] = qkv_ref[...]
        scext_ref[pad:pad + tile, :] = sc_ref[...] * sh_ref[...]

    rb = min(tile, 64)
    convw = convw_ref[...]
    scw = scw_ref[...]
    for r in range(tile // rb):
        base = pad + r * rb
        acc = ext_ref[base - 3:base - 3 + rb, :] * convw[0:1, :]
        for j in range(1, GDN_CONV):
            acc = acc + ext_ref[base - 3 + j:base - 3 + j + rb, :] * convw[j:j + 1, :]
        act_ref[r * rb:(r + 1) * rb, :] = _silu(acc)
        u = scext_ref[base - 2:base - 2 + rb, :] * scw[0:1, :]
        for j in range(1, SC_CONV):
            u = u + scext_ref[base - 2 + j:base - 2 + j + rb, :] * scw[j:j + 1, :]
        yb_ref[r * rb:(r + 1) * rb, :] = sb_ref[r * rb:(r + 1) * rb, :] * u

    @pl.when(t == nt - 1)
    def _():
        gbuf_out_ref[...] = ext_ref[pad + valid - (GDN_CONV - 1):pad + valid, :]
        sbuf_out_ref[...] = scext_ref[pad + valid - (SC_CONV - 1):pad + valid, :]

    ext_ref[0:pad, :] = ext_ref[tile:tile + pad, :]
    scext_ref[0:pad, :] = scext_ref[tile:tile + pad, :]

    c = chunk
    nc = tile // c
    ri = lax.broadcasted_iota(jnp.int32, (c, c), 0)
    ci = lax.broadcasted_iota(jnp.int32, (c, c), 1)
    incl = ri >= ci
    strict = ri > ci
    tri = jnp.where(incl, 1.0, 0.0).astype(BF16)
    tri_t = jnp.where(ri <= ci, 1.0, 0.0).astype(BF16)
    neg_a = -jnp.exp(alog_ref[...])
    dtb = dtb_ref[...]
    normw = normw_ref[...]
    nlev = max(1, int(math.ceil(math.log2(c))))
    rowmask = None
    if valid < tile:
        rowmask = jnp.where(lax.broadcasted_iota(jnp.int32, (c, 1), 0) < valid, 1.0, 0.0)
    heads = range(GDN_HEADS)

    beta_blks, cums, cum_ts = [], [], []
    for ic in range(nc):
        abc = ab_ref[ic * c:(ic + 1) * c, :]
        beta_blk = jax.nn.sigmoid(abc)
        xg = abc + dtb
        g_blk = neg_a * (jnp.maximum(xg, 0.0) + jnp.log1p(jnp.exp(-jnp.abs(xg))))
        if rowmask is not None:
            beta_blk = beta_blk * rowmask
            g_blk = g_blk * rowmask
        g1, g2, g3 = _split3(g_blk)
        beta_blks.append(beta_blk)
        cums.append(_nn(tri, g1) + _nn(tri, g2) + _nn(tri, g3))
        cum_ts.append(_tn(g1, tri_t) + _tn(g2, tri_t) + _tn(g3, tri_t))

    pairs = [(ic, h) for ic in range(nc) for h in heads]
    npairs = len(pairs)
    qs, ks, vs, betas, gcs, decs, kks, qks = [], [], [], [], [], [], [], []
    for ic, h in pairs:
        rows = slice(ic * c, (ic + 1) * c)
        q = act_ref[rows, h * GDN_DK:(h + 1) * GDN_DK]
        k = act_ref[rows, GDN_QK + h * GDN_DK:GDN_QK + (h + 1) * GDN_DK]
        v = act_ref[rows, 2 * GDN_QK + h * GDN_DV:2 * GDN_QK + (h + 1) * GDN_DV]
        q = q * lax.rsqrt(jnp.sum(q * q, axis=-1, keepdims=True) + RMS_EPS) * (GDN_DK ** -0.5)
        k = k * lax.rsqrt(jnp.sum(k * k, axis=-1, keepdims=True) + RMS_EPS)
        if rowmask is not None:
            k = k * rowmask
            v = v * rowmask
        gc = cums[ic][:, GDN_HEADS + h:GDN_HEADS + h + 1]
        gr = cum_ts[ic][GDN_HEADS + h:GDN_HEADS + h + 1, :]
        k1, k2 = _split2(k)
        qs.append(q)
        ks.append(k)
        vs.append(v)
        betas.append(beta_blks[ic][:, h:h + 1])
        gcs.append(gc)
        decs.append(jnp.where(incl, jnp.exp(jnp.where(incl, gc - gr, 0.0)), 0.0))
        kks.append(_nt(k1, k1) + _nt(k1, k2) + _nt(k2, k1))
        qks.append(_nt(q.astype(BF16), k1))
    xs = [-jnp.where(strict, betas[p] * kks[p] * decs[p], 0.0) for p in range(npairs)]
    ams = list(xs)
    for _ in range(nlev - 1):
        xbs = [x.astype(BF16) for x in xs]
        xs = [_nn(xb, xb) for xb in xbs]
        ams = [am + x + _nn(am.astype(BF16), x.astype(BF16)) for am, x in zip(ams, xs)]
    egs = [jnp.exp(gc) for gc in gcs]
    rhss = [jnp.concatenate([betas[p] * vs[p], (betas[p] * egs[p]) * ks[p]], axis=-1) for p in range(npairs)]
    sols = [rhs + _nn(am.astype(BF16), rhs.astype(BF16)) for am, rhs in zip(ams, rhss)]
    u_bars = [sol[:, :GDN_DV] for sol in sols]
    wq = [jnp.concatenate([sols[p][:, GDN_DV:], egs[p] * qs[p]], axis=0).astype(BF16) for p in range(npairs)]
    qkb = [(qks[p] * decs[p]).astype(BF16) for p in range(npairs)]
    g_lasts = [gc[c - 1:c, :] for gc in gcs]
    k_decs = [(jnp.exp(g_lasts[p] - gcs[p]) * ks[p]).astype(BF16) for p in range(npairs)]

    states = [state_ref[h] for h in heads]
    for ic in range(nc):
        rows = slice(ic * c, (ic + 1) * c)
        sbs = [s.astype(BF16) for s in states]
        ws = [_nn(wq[ic * GDN_HEADS + h], sbs[h]) for h in heads]
        ubs = [(u_bars[ic * GDN_HEADS + h] - ws[h][:c]).astype(BF16) for h in heads]
        o2 = [_nn(qkb[ic * GDN_HEADS + h], ubs[h]) for h in heads]
        ds = [_tn(k_decs[ic * GDN_HEADS + h], ubs[h]) for h in heads]
        for h in heads:
            p = ic * GDN_HEADS + h
            states[h] = jnp.exp(g_lasts[p]) * states[h] + ds[h]
            o = ws[h][c:] + o2[h]
            zz = z_ref[rows, h * GDN_DV:(h + 1) * GDN_DV]
            ya_ref[rows, h * GDN_DV:(h + 1) * GDN_DV] = _rms_norm(o, normw) * _silu(zz)
    for h in heads:
        state_ref[h] = states[h]

    @pl.when(t == nt - 1)
    def _():
        sout_ref[...] = state_ref[...]


def _recurrent(src, w_rec, s0, gbuf0, sbuf0, convw, alog_row, dtb_row, normw, scw, *, tile, chunk, valid):
    b, l, _ = src.shape
    nt = l // tile
    project = w_rec is not None

    def col(width, start):
        return pl.BlockSpec((None, tile, width), lambda i, t: (i, t, start // width))

    def per_b(shape):
        return pl.BlockSpec((None,) + shape, lambda i, t: (i,) + (0,) * len(shape))

    if project:
        src_specs = [pl.BlockSpec((None, tile, D_MODEL), lambda i, t: (i, t, 0)), _const_spec((D_MODEL, REC_COLS))]
        src_args = [src, w_rec]
        extra_scratch = [pltpu.VMEM((tile, GDN_V), F32), pltpu.VMEM((tile, SC_WIDTH), F32),
                         pltpu.VMEM((tile, LANE), F32)]
    else:
        src_specs = [col(GDN_CONV_CH, COL_QKV), col(GDN_V, COL_Z), col(SC_WIDTH, COL_SB), col(SC_WIDTH, COL_SC),
                     col(SC_WIDTH, COL_SH), col(LANE, COL_AB)]
        src_args = [src] * 6
        extra_scratch = []
    pad = SUBLANE
    kern = functools.partial(_recurrent_kernel, tile=tile, chunk=chunk, valid=valid, project=project)
    return pl.pallas_call(
        kern,
        grid=(b, nt),
        in_specs=src_specs + [
            per_b((GDN_HEADS, GDN_DK, GDN_DV)), per_b((GDN_CONV - 1, GDN_CONV_CH)), per_b((SC_CONV - 1, SC_WIDTH)),
            _const_spec((GDN_CONV, GDN_CONV_CH)), _const_spec((1, LANE)), _const_spec((1, LANE)),
            _const_spec((1, GDN_DV)), _const_spec((SC_CONV, SC_WIDTH))],
        out_specs=[pl.BlockSpec((None, tile, GDN_V), lambda i, t: (i, t, 0)),
                   pl.BlockSpec((None, tile, SC_WIDTH), lambda i, t: (i, t, 0)),
                   per_b((GDN_HEADS, GDN_DK, GDN_DV)), per_b((GDN_CONV - 1, GDN_CONV_CH)),
                   per_b((SC_CONV - 1, SC_WIDTH))],
        out_shape=[jax.ShapeDtypeStruct((b, l, GDN_V), F32), jax.ShapeDtypeStruct((b, l, SC_WIDTH), F32),
                   jax.ShapeDtypeStruct((b, GDN_HEADS, GDN_DK, GDN_DV), F32),
                   jax.ShapeDtypeStruct((b, GDN_CONV - 1, GDN_CONV_CH), F32),
                   jax.ShapeDtypeStruct((b, SC_CONV - 1, SC_WIDTH), F32)],
        scratch_shapes=[pltpu.VMEM((tile + pad, GDN_CONV_CH), F32), pltpu.VMEM((tile + pad, SC_WIDTH), F32),
                        pltpu.VMEM((tile, GDN_CONV_CH), F32), pltpu.VMEM((GDN_HEADS, GDN_DK, GDN_DV), F32)]
                       + extra_scratch,
        compiler_params=pltpu.CompilerParams(
            dimension_semantics=("parallel", "arbitrary"), vmem_limit_bytes=VMEM_LIMIT),
        name="recurrent",
    )(*src_args, s0, gbuf0, sbuf0, convw, alog_row, dtb_row, normw, scw)


def _diff_lambda(lam_ref, lam_init):
    lq = lam_ref[...]
    a = jnp.sum(lq[0:1, :] * lq[1:2, :], axis=-1, keepdims=True)
    b = jnp.sum(lq[2:3, :] * lq[3:4, :], axis=-1, keepdims=True)
    return jnp.exp(a) - jnp.exp(b) + lam_init


def _softmax_parts(s_off, s_diag):
    m = jnp.max(s_diag, axis=-1, keepdims=True)
    if s_off is not None:
        m = jnp.maximum(m, jnp.max(s_off, axis=-1, keepdims=True))
    e_diag = jnp.exp2(s_diag - m)
    total = jnp.sum(e_diag, axis=-1, keepdims=True)
    e_off = None
    if s_off is not None:
        e_off = jnp.exp2(s_off - m)
        total = total + jnp.sum(e_off, axis=-1, keepdims=True)
    return e_off, e_diag, total


def _attn_prompt_kernel(*refs, tq, nt, lam_init, aliased):
    x_ref, w_ref, lam_ref, nw_ref = refs[:4]
    refs = refs[4 + (2 if aliased else 0):]
    yc_ref, krow_ref, vrow_ref, q_s, k_s, v_s, o_s = refs
    t = pl.program_id(1)
    xb = x_ref[...].astype(BF16)
    q = _nn(xb, w_ref[:, 0:DIFF_QK])
    k = _nn(xb, w_ref[:, DIFF_QK:2 * DIFF_QK])
    v = _nn(xb, w_ref[:, 2 * DIFF_QK:ATT_COLS])
    r0 = pl.multiple_of(t * tq, tq)
    lane = lax.broadcasted_iota(jnp.int32, (1, 2 * DIFF_HD), 1)
    for h in range(DIFF_HEADS):
        qh = q[:, h * LANE:(h + 1) * LANE]
        kh = k[:, h * LANE:(h + 1) * LANE]
        vh = v[:, h * LANE:(h + 1) * LANE]
        krow_ref[pl.ds(h, tq, stride=DIFF_HEADS), :] = kh
        vrow_ref[pl.ds(h, tq, stride=DIFF_HEADS), :] = vh
        q_s[h, 0:tq, :] = jnp.where(lane < DIFF_HD, qh, 0.0).astype(BF16)
        q_s[h, tq:2 * tq, :] = jnp.where(lane >= DIFF_HD, qh, 0.0).astype(BF16)
        k_s[h, pl.ds(r0, tq), :] = kh.astype(BF16)
        v_s[h, pl.ds(r0, tq), :] = vh.astype(BF16)
    lam = _diff_lambda(lam_ref, lam_init)
    nw = nw_ref[...]
    row = lax.broadcasted_iota(jnp.int32, (2 * tq, tq), 0)
    causal = lax.broadcasted_iota(jnp.int32, (2 * tq, tq), 1) <= jnp.where(row < tq, row, row - tq)
    sc = DIFF_SCALE * math.log2(math.e)
    group = 2

    for i in range(nt):
        off = i * tq

        @pl.when(t == i)
        def _(off=off):
            def group_body(g, carry):
                hs = [g * group + u for u in range(group)]
                qs = [q_s[h] for h in hs]
                s_diag = [jnp.where(causal, _nt(qs[u], k_s[h, off:off + tq, :]) * sc, NEG_BIG)
                          for u, h in enumerate(hs)]
                s_off = [_nt(qs[u], k_s[h, 0:off, :]) * sc if off else None for u, h in enumerate(hs)]
                parts = [_softmax_parts(so, sd) for so, sd in zip(s_off, s_diag)]
                outs = [_nn(parts[u][1].astype(BF16), v_s[h, off:off + tq, :]) for u, h in enumerate(hs)]
                if off:
                    outs = [outs[u] + _nn(parts[u][0].astype(BF16), v_s[h, 0:off, :]) for u, h in enumerate(hs)]
                for u, h in enumerate(hs):
                    on = outs[u] * (1.0 / parts[u][2])
                    o = on[0:tq] - lam * on[tq:2 * tq]
                    o_s[h] = _rms_norm(o, nw) * (1.0 - lam_init)
                return carry

            lax.fori_loop(0, DIFF_HEADS // group, group_body, 0)

    for h in range(DIFF_HEADS):
        yc_ref[:, h * DIFF_VD:(h + 1) * DIFF_VD] = o_s[h]


def _attn_prompt(x3, w_att, lam_p, nw, lam_init, tq, layer, depth, kv_prev):
    b, l, _ = x3.shape
    nt = l // tq
    aliased = kv_prev is not None
    kern = functools.partial(_attn_prompt_kernel, tq=tq, nt=nt, lam_init=lam_init, aliased=aliased)
    row = lambda w: pl.BlockSpec((None, tq, w), lambda i, t: (i, t, 0))
    kv_row = pl.BlockSpec((None, None, tq * DIFF_HEADS, DIFF_VD), lambda i, t: (i, layer, t, 0))
    kv_shape = jax.ShapeDtypeStruct((b, depth, l * DIFF_HEADS, DIFF_VD), F32)
    in_specs = [row(D_MODEL), _const_spec((D_MODEL, ATT_COLS)), _const_spec((4, DIFF_HD)), _const_spec((1, DIFF_VD))]
    args = [x3, w_att, lam_p, nw]
    aliases = {}
    if aliased:
        in_specs += [pl.BlockSpec(memory_space=pl.ANY)] * 2
        args += list(kv_prev)
        aliases = {4: 1, 5: 2}
    return pl.pallas_call(
        kern,
        grid=(b, nt),
        in_specs=in_specs,
        out_specs=[row(DIFF_V), kv_row, kv_row],
        out_shape=[jax.ShapeDtypeStruct((b, l, DIFF_V), F32), kv_shape, kv_shape],
        scratch_shapes=[pltpu.VMEM((DIFF_HEADS, 2 * tq, 2 * DIFF_HD), BF16),
                        pltpu.VMEM((DIFF_HEADS, l, 2 * DIFF_HD), BF16), pltpu.VMEM((DIFF_HEADS, l, DIFF_VD), BF16),
                        pltpu.VMEM((DIFF_HEADS, tq, DIFF_VD), F32)],
        input_output_aliases=aliases,
        compiler_params=pltpu.CompilerParams(
            dimension_semantics=("parallel", "arbitrary"), vmem_limit_bytes=VMEM_LIMIT),
        name="attn_prompt",
    )(*args)


def _attn_sample_kernel(pt_ref, q_ref, kn_ref, vn_ref, lam_ref, nw_ref, *rest, pages, lam_init):
    k_refs = rest[:pages]
    v_refs = rest[pages:2 * pages]
    o_ref = rest[2 * pages]
    m_ref, l_ref, acc_ref = rest[2 * pages + 1:]
    j = pl.program_id(1)
    nj = pl.num_programs(1)
    rows = DIFF_HEADS * 2 * 4
    rpp = PAGE_SIZE * DIFF_HEADS

    @pl.when(j == 0)
    def _():
        m_ref[...] = jnp.full((rows, 1), NEG_BIG, F32)
        l_ref[...] = jnp.zeros((rows, 1), F32)
        acc_ref[...] = jnp.zeros((rows, DIFF_VD), F32)

    q = q_ref[...]

    def update(s, pv):
        m_old = m_ref[...]
        m_new = jnp.maximum(m_old, jnp.max(s, axis=-1, keepdims=True))
        alpha = jnp.exp(m_old - m_new)
        p = jnp.exp(s - m_new)
        l_ref[...] = alpha * l_ref[...] + jnp.sum(p, axis=-1, keepdims=True)
        acc_ref[...] = alpha * acc_ref[...] + pv(p)
        m_ref[...] = m_new

    def head_rows(page_refs, h):
        return jnp.concatenate([r[pl.ds(h, PAGE_SIZE, stride=DIFF_HEADS), :].astype(BF16) for r in page_refs], axis=0)

    rph = rows // DIFF_HEADS
    s = jnp.concatenate([_nt(q[h * rph:(h + 1) * rph].astype(BF16), head_rows(k_refs, h))
                         for h in range(DIFF_HEADS)], axis=0) * DIFF_SCALE

    def pv_past(p):
        return jnp.concatenate([_nn(p[h * rph:(h + 1) * rph].astype(BF16), head_rows(v_refs, h))
                                for h in range(DIFF_HEADS)], axis=0)

    update(s, pv_past)

    @pl.when(j == nj - 1)
    def _():
        nk = kn_ref.shape[0]
        r = lax.broadcasted_iota(jnp.int32, (rows, nk), 0)
        cc = lax.broadcasted_iota(jnp.int32, (rows, nk), 1)
        ok = ((cc & (DIFF_HEADS - 1)) == (r >> 3)) & ((cc >> 2) <= (r & 3))
        sn = jnp.where(ok, _nt(q.astype(BF16), kn_ref[...].astype(BF16)) * DIFF_SCALE, NEG_BIG)
        vnb = vn_ref[...].astype(BF16)
        update(sn, lambda p: _nn(p.astype(BF16), vnb))
        lam = _diff_lambda(lam_ref, lam_init)
        o = acc_ref[...] / l_ref[...]
        nw = nw_ref[...]
        for h in range(DIFF_HEADS):
            blk = o[h * 8:(h + 1) * 8, :]
            d = blk - lam * pltpu.roll(blk, 4, 0)
            o_ref[h] = _rms_norm(d, nw) * (1.0 - lam_init)


def _attn_sample(q_rows, k_new, v_new, lam_p, nw, cache_k4, cache_v4, page_table, layer, lam_init, pages):
    b = q_rows.shape[0]
    n_pages = page_table.shape[1]
    rows = q_rows.shape[1]
    rpp = PAGE_SIZE * DIFF_HEADS

    def page_spec(i):
        return pl.BlockSpec((None, None, rpp, DIFF_VD), lambda bi, j, pt: (pt[bi, j * pages + i], layer, 0, 0))

    def per_b(shape):
        return pl.BlockSpec((None,) + shape, lambda bi, j, pt: (bi,) + (0,) * len(shape))

    kern = functools.partial(_attn_sample_kernel, pages=pages, lam_init=lam_init)
    grid_spec = pltpu.PrefetchScalarGridSpec(
        num_scalar_prefetch=1,
        grid=(b, n_pages // pages),
        in_specs=[per_b((rows, 2 * DIFF_HD)), per_b(k_new.shape[1:]), per_b(v_new.shape[1:]),
                  pl.BlockSpec((4, DIFF_HD), lambda bi, j, pt: (0, 0)),
                  pl.BlockSpec((1, DIFF_VD), lambda bi, j, pt: (0, 0))]
                 + [page_spec(i) for i in range(pages)] + [page_spec(i) for i in range(pages)],
        out_specs=per_b((DIFF_HEADS, 8, DIFF_VD)),
        scratch_shapes=[pltpu.VMEM((rows, 1), F32), pltpu.VMEM((rows, 1), F32), pltpu.VMEM((rows, DIFF_VD), F32)],
    )
    return pl.pallas_call(
        kern,
        grid_spec=grid_spec,
        out_shape=jax.ShapeDtypeStruct((b, DIFF_HEADS, 8, DIFF_VD), F32),
        compiler_params=pltpu.CompilerParams(
            dimension_semantics=("parallel", "arbitrary"), vmem_limit_bytes=VMEM_LIMIT),
        name="attn_sample",
    )(page_table, q_rows, k_new, v_new, lam_p, nw, *([cache_k4] * pages), *([cache_v4] * pages))


def _merge_mlp_kernel(x_ref, ya_ref, yb_ref, yc_ref, wg_ref, wb_ref, wo_ref, l1g_ref, l1b_ref,
                      wu_ref, wd_ref, l2g_ref, l2b_ref, o_ref, *, alpha, ff_chunk):
    x = x_ref[...]
    xb = x.astype(BF16)
    merged = None
    for i, y_ref in enumerate((ya_ref, yb_ref, yc_ref)):
        gate = jax.nn.sigmoid(_nn(xb, wg_ref[:, i * D_MODEL:(i + 1) * D_MODEL]))
        term = gate * _nn(y_ref[...].astype(BF16), wb_ref[i])
        merged = term if merged is None else merged + term
    mix = _nn(merged.astype(BF16), wo_ref[...])
    x1 = _layer_norm(alpha * x + mix, l1g_ref[...], l1b_ref[...])
    x1b = x1.astype(BF16)
    acc = None
    for cidx in range(D_FF // ff_chunk):
        up = jnp.maximum(_nn(x1b, wu_ref[:, cidx * ff_chunk:(cidx + 1) * ff_chunk]), 0.0)
        term = _nn((up * up).astype(BF16), wd_ref[cidx * ff_chunk:(cidx + 1) * ff_chunk, :])
        acc = term if acc is None else acc + term
    o_ref[...] = _layer_norm(alpha * x1 + acc, l2g_ref[...], l2b_ref[...])


def _merge_mlp(x2, ya2, yb2, yc2, wl, tm, alpha):
    n = x2.shape[0]
    row = lambda w: pl.BlockSpec((tm, w), lambda i: (i, 0))
    return pl.pallas_call(
        functools.partial(_merge_mlp_kernel, alpha=alpha, ff_chunk=1024),
        grid=(n // tm,),
        in_specs=[row(D_MODEL), row(BRANCH_W), row(BRANCH_W), row(BRANCH_W),
                  _const_spec((D_MODEL, N_BRANCH * D_MODEL)), _const_spec((N_BRANCH, BRANCH_W, D_MODEL)),
                  _const_spec((D_MODEL, D_MODEL)), _const_spec((1, D_MODEL)), _const_spec((1, D_MODEL)),
                  _const_spec((D_MODEL, D_FF)), _const_spec((D_FF, D_MODEL)),
                  _const_spec((1, D_MODEL)), _const_spec((1, D_MODEL))],
        out_specs=row(D_MODEL),
        out_shape=jax.ShapeDtypeStruct((n, D_MODEL), F32),
        compiler_params=pltpu.CompilerParams(dimension_semantics=("parallel",), vmem_limit_bytes=VMEM_LIMIT),
        name="merge_mlp",
    )(x2, ya2, yb2, yc2, wl["w_gates"], wl["w_branch"], wl["w_o"], wl["ln1_g"], wl["ln1_b"],
      wl["w_up"], wl["w_down"], wl["ln2_g"], wl["ln2_b"])


def _regroup_w_in(w):
    o_z = GDN_CONV_CH
    o_ab = o_z + GDN_V
    o_sb = o_ab + 2 * GDN_HEADS
    o_dq = o_sb + 3 * SC_WIDTH
    o_gates = o_dq + 2 * DIFF_QK + DIFF_V
    zeros = jnp.zeros((w.shape[0], LANE - 2 * GDN_HEADS), w.dtype)
    w_h = jnp.concatenate([w[:, :o_ab], w[:, o_sb:o_dq], w[:, o_ab:o_sb], zeros, w[:, o_dq:o_gates]], axis=1)
    return w_h.astype(BF16), w[:, o_gates:].astype(BF16)


def _lane_row(vals, offset):
    n = vals.shape[0]
    return jnp.pad(vals.astype(F32), (offset, LANE - offset - n)).reshape(1, LANE)


def kernel(x_prompt, x_sample, cache_k, cache_v, page_table, state_gdn, state_gdn_conv, state_sc_conv, w_in,
           gdn_conv_w, gdn_a_log, gdn_dt_bias, gdn_norm_w, sc_conv_w, diff_lambda, diff_norm_w, w_branch, w_o,
           ln1_g, ln1_b, ln2_g, ln2_b, w_up, w_down):
    depth = w_in.shape[0]
    alpha = (2 * depth) ** 0.25
    b_p, seq, _ = x_prompt.shape
    b_s, dec_seq, _ = x_sample.shape
    n_phys = cache_k.shape[0]
    assert dec_seq == 4 and seq % 256 == 0
    dec_pad = SUBLANE
    rpp = PAGE_SIZE * DIFF_HEADS
    cache_k4 = cache_k.reshape(n_phys, depth, rpp, 2 * DIFF_HD)
    cache_v4 = cache_v.reshape(n_phys, depth, rpp, DIFF_VD)
    prompt_tile = 256
    tok_tile = 256
    sample_pages = 32
    assert page_table.shape[1] % sample_pages == 0

    xp = x_prompt
    xs = jnp.pad(x_sample, ((0, 0), (0, dec_pad - dec_seq), (0, 0)))
    half = (jnp.arange(2 * DIFF_HD) < DIFF_HD)
    map_mask = jnp.stack([half, ~half]).astype(F32)
    rows_p = [[], [], [], [], []]
    rows_s = [[], [], [], [], []]
    kv_prompt = None
    for l in range(depth):
        lam_init = 0.8 - 0.6 * math.exp(-0.3 * l)
        w_h, w_gates = _regroup_w_in(w_in[l])
        wl = {
            "w_gates": w_gates,
            "w_branch": w_branch[l].astype(BF16), "w_o": w_o[l].astype(BF16),
            "ln1_g": ln1_g[l].reshape(1, D_MODEL), "ln1_b": ln1_b[l].reshape(1, D_MODEL),
            "ln2_g": ln2_g[l].reshape(1, D_MODEL), "ln2_b": ln2_b[l].reshape(1, D_MODEL),
            "w_up": w_up[l].astype(BF16), "w_down": w_down[l].astype(BF16),
        }
        rec_w = (gdn_conv_w[l], _lane_row(gdn_a_log[l], GDN_HEADS), _lane_row(gdn_dt_bias[l], GDN_HEADS),
                 gdn_norm_w[l].reshape(1, GDN_DV), sc_conv_w[l])
        lam_p = diff_lambda[l].astype(F32)
        nw = diff_norm_w[l].reshape(1, DIFF_VD)

        ya, yb, s_p, gbuf_p, sbuf_p = _recurrent(
            xp, w_h[:, :REC_COLS], jnp.zeros((b_p, GDN_HEADS, GDN_DK, GDN_DV), F32),
            jnp.zeros((b_p, GDN_CONV - 1, GDN_CONV_CH), F32), jnp.zeros((b_p, SC_CONV - 1, SC_WIDTH), F32),
            *rec_w, tile=prompt_tile, chunk=GDN_CHUNK, valid=prompt_tile)
        yc, *kv_prompt = _attn_prompt(xp, w_h[:, REC_COLS:], lam_p, nw, lam_init, prompt_tile, l, depth, kv_prompt)
        n_p = b_p * seq
        xp = _merge_mlp(xp.reshape(n_p, D_MODEL), ya.reshape(n_p, GDN_V), yb.reshape(n_p, SC_WIDTH),
                        yc.reshape(n_p, DIFF_V), wl, tok_tile, alpha).reshape(b_p, seq, D_MODEL)
        rows_p[2].append(s_p)
        rows_p[3].append(gbuf_p)
        rows_p[4].append(sbuf_p)

        n_s = b_s * dec_pad
        h3s = _in_proj(xs.reshape(n_s, D_MODEL), w_h).reshape(b_s, dec_pad, H_COLS)
        ya, yb, s_s, gbuf_s, sbuf_s = _recurrent(
            h3s, None, state_gdn[:, l], state_gdn_conv[:, l], state_sc_conv[:, l],
            *rec_w, tile=dec_pad, chunk=dec_pad, valid=dec_seq)
        q = h3s[:, :dec_seq, COL_DQ:COL_DQ + DIFF_QK].reshape(b_s, dec_seq, DIFF_HEADS, 2 * DIFF_HD)
        q = jnp.transpose(q, (0, 2, 1, 3))[:, :, None] * map_mask[None, None, :, None, :]
        q_rows = q.reshape(b_s, DIFF_HEADS * 2 * dec_seq, 2 * DIFF_HD)
        k_new = h3s[:, :dec_seq, COL_DK:COL_DK + DIFF_QK].reshape(b_s, dec_seq * DIFF_HEADS, 2 * DIFF_HD)
        v_new = h3s[:, :dec_seq, COL_DV:COL_DV + DIFF_V].reshape(b_s, dec_seq * DIFF_HEADS, DIFF_VD)
        o = _attn_sample(q_rows, jnp.pad(k_new, ((0, 0), (0, LANE - dec_seq * DIFF_HEADS), (0, 0))),
                         jnp.pad(v_new, ((0, 0), (0, LANE - dec_seq * DIFF_HEADS), (0, 0))),
                         lam_p, nw, cache_k4, cache_v4, page_table, l, lam_init, pages=sample_pages)
        yc = jnp.transpose(o[:, :, :dec_seq, :], (0, 2, 1, 3)).reshape(b_s, dec_seq, DIFF_V)
        yc = jnp.pad(yc, ((0, 0), (0, dec_pad - dec_seq), (0, 0)))
        xs = _merge_mlp(xs.reshape(n_s, D_MODEL), ya.reshape(n_s, GDN_V), yb.reshape(n_s, SC_WIDTH),
                        yc.reshape(n_s, DIFF_V), wl, n_s, alpha).reshape(b_s, dec_pad, D_MODEL)
        rows_s[0].append(k_new.reshape(b_s, dec_seq, DIFF_HEADS, 2 * DIFF_HD))
        rows_s[1].append(v_new.reshape(b_s, dec_seq, DIFF_HEADS, DIFF_VD))
        rows_s[2].append(s_s)
        rows_s[3].append(gbuf_s)
        rows_s[4].append(sbuf_s)

    outs_p = [a.reshape(b_p, depth, seq, DIFF_HEADS, DIFF_VD) for a in kv_prompt]
    outs_p += [jnp.stack(r, axis=1) for r in rows_p[2:]]
    outs_s = [jnp.stack(r, axis=1) for r in rows_s]
    return (xp, xs[:, :dec_seq], *outs_p, *outs_s)
```

```python
import functools
import math

import jax
import jax.numpy as jnp
from jax import lax
from jax.experimental import pallas as pl
from jax.experimental.pallas import tpu as pltpu

F32 = jnp.float32
BF16 = jnp.bfloat16

D_MODEL = 1024
GDN_HEADS = 4
GDN_DK = 128
GDN_DV = 128
GDN_CONV = 4
GDN_QK = GDN_HEADS * GDN_DK
GDN_V = GDN_HEADS * GDN_DV
GDN_CONV_CH = 2 * GDN_QK + GDN_V
GDN_CHUNK = 64
SC_WIDTH = 512
SC_CONV = 3
DIFF_HEADS = 4
DIFF_HD = 64
DIFF_VD = 128
DIFF_QK = DIFF_HEADS * 2 * DIFF_HD
DIFF_V = DIFF_HEADS * DIFF_VD
DIFF_SCALE = DIFF_HD ** -0.5
PAGE_SIZE = 128
N_BRANCH = 3
BRANCH_W = 512
D_FF = 4 * D_MODEL
LN_EPS = 1e-5
RMS_EPS = 1e-6
NEG_BIG = -1e30

LANE = 128
SUBLANE = 8

COL_QKV = 0
COL_Z = COL_QKV + GDN_CONV_CH
COL_SB = COL_Z + GDN_V
COL_SC = COL_SB + SC_WIDTH
COL_SH = COL_SC + SC_WIDTH
COL_AB = COL_SH + SC_WIDTH
REC_COLS = COL_AB + LANE
ATT_COLS = DIFF_QK + DIFF_QK + DIFF_V

VMEM_LIMIT = 56 * 1024 * 1024


def _nt(a, b):
    return lax.dot_general(a, b, (((1,), (1,)), ((), ())), preferred_element_type=F32)


def _tn(a, b):
    return lax.dot_general(a, b, (((0,), (0,)), ((), ())), preferred_element_type=F32)


def _nn(a, b):
    return jnp.dot(a, b, preferred_element_type=F32)


def _split3(x):
    x1 = x.astype(BF16)
    r1 = x - x1.astype(F32)
    x2 = r1.astype(BF16)
    r2 = r1 - x2.astype(F32)
    return x1, x2, r2.astype(BF16)


def _silu(x):
    return x * jax.nn.sigmoid(x)


def _layer_norm(x, g, b):
    mu = jnp.mean(x, axis=-1, keepdims=True)
    xc = x - mu
    var = jnp.mean(xc * xc, axis=-1, keepdims=True)
    return xc * lax.rsqrt(var + LN_EPS) * g + b


def _rms_norm(x, w):
    return x * lax.rsqrt(jnp.mean(x * x, axis=-1, keepdims=True) + RMS_EPS) * w


def _const_spec(shape):
    return pl.BlockSpec(shape, lambda *_: (0,) * len(shape), pipeline_mode=pl.Buffered(1))


def _in_proj_kernel(x_ref, wr_ref, wa_ref, hr_ref, ha_ref):
    xb = x_ref[...].astype(BF16)
    hr_ref[...] = _nn(xb, wr_ref[...])
    ha_ref[...] = _nn(xb, wa_ref[...])


def _in_proj(x2, w_rec, w_att):
    n = x2.shape[0]
    full = lambda shape: pl.BlockSpec(shape, lambda i: (0, 0))
    return pl.pallas_call(
        _in_proj_kernel,
        grid=(1,),
        in_specs=[full((n, D_MODEL)), full((D_MODEL, REC_COLS)), full((D_MODEL, ATT_COLS))],
        out_specs=[full((n, REC_COLS)), full((n, ATT_COLS))],
        out_shape=[jax.ShapeDtypeStruct((n, REC_COLS), F32), jax.ShapeDtypeStruct((n, ATT_COLS), F32)],
        compiler_params=pltpu.CompilerParams(dimension_semantics=("arbitrary",), vmem_limit_bytes=VMEM_LIMIT),
        name="in_proj",
    )(x2, w_rec, w_att)


def _recurrent_kernel(*refs, nb, tile, chunk, valid, project):
    if project:
        x_ref, w_ref = refs[:2]
        refs = refs[2:]
    else:
        qkv_ref, z_ref, sb_ref, sc_ref, sh_ref, ab_ref = refs[:6]
        refs = refs[6:]
    (s0_ref, gbuf0_ref, sbuf0_ref, convw_ref, alog_ref, dtb_ref, normw_ref, scw_ref,
     ya_ref, yb_ref, sout_ref, gbuf_out_ref, sbuf_out_ref,
     ext_ref, scext_ref, act_ref, state_ref) = refs[:17]
    t = pl.program_id(1)
    nt = pl.num_programs(1)
    pad = SUBLANE
    seqs = range(nb)

    @pl.when(t == 0)
    def _():
        state_ref[...] = s0_ref[...]
        for bb in seqs:
            ext_ref[bb, 0:pad, :] = jnp.zeros((pad, GDN_CONV_CH), F32)
            ext_ref[bb, pad - (GDN_CONV - 1):pad, :] = gbuf0_ref[bb]
            scext_ref[bb, 0:pad, :] = jnp.zeros((pad, SC_WIDTH), F32)
            scext_ref[bb, pad - (SC_CONV - 1):pad, :] = sbuf0_ref[bb]

    if project:
        z_ref, sb_ref, ab_ref = refs[17:20]
        xb = x_ref[...].reshape(nb * tile, D_MODEL).astype(BF16)

        def proj(lo, hi):
            return _nn(xb, w_ref[:, lo:hi]).reshape(nb, tile, hi - lo)

        ext_ref[:, pad:pad + tile, :] = proj(COL_QKV, COL_Z)
        z_ref[...] = proj(COL_Z, COL_SB)
        sb_ref[...] = proj(COL_SB, COL_SC)
        scext_ref[:, pad:pad + tile, :] = proj(COL_SC, COL_SH) * proj(COL_SH, COL_AB)
        ab_ref[...] = proj(COL_AB, REC_COLS)
    else:
        ext_ref[:, pad:pad + tile, :] = qkv_ref[...]
        scext_ref[:, pad:pad + tile, :] = sc_ref[...] * sh_ref[...]

    rb = min(tile, 64)
    convw = convw_ref[...]
    scw = scw_ref[...]
    for bb in seqs:
        for r in range(tile // rb):
            base = pad + r * rb
            acc = ext_ref[bb, base - 3:base - 3 + rb, :] * convw[0:1, :]
            for j in range(1, GDN_CONV):
                acc = acc + ext_ref[bb, base - 3 + j:base - 3 + j + rb, :] * convw[j:j + 1, :]
            act_ref[bb, r * rb:(r + 1) * rb, :] = _silu(acc)
            u = scext_ref[bb, base - 2:base - 2 + rb, :] * scw[0:1, :]
            for j in range(1, SC_CONV):
                u = u + scext_ref[bb, base - 2 + j:base - 2 + j + rb, :] * scw[j:j + 1, :]
            yb_ref[bb, r * rb:(r + 1) * rb, :] = sb_ref[bb, r * rb:(r + 1) * rb, :] * u

    @pl.when(t == nt - 1)
    def _():
        gbuf_out_ref[...] = ext_ref[:, pad + valid - (GDN_CONV - 1):pad + valid, :]
        sbuf_out_ref[...] = scext_ref[:, pad + valid - (SC_CONV - 1):pad + valid, :]

    ext_ref[:, 0:pad, :] = ext_ref[:, tile:tile + pad, :]
    scext_ref[:, 0:pad, :] = scext_ref[:, tile:tile + pad, :]

    c = chunk
    nc = tile // c
    ri = lax.broadcasted_iota(jnp.int32, (c, c), 0)
    ci = lax.broadcasted_iota(jnp.int32, (c, c), 1)
    incl = ri >= ci
    strict = ri > ci
    tri = jnp.where(incl, 1.0, 0.0).astype(BF16)
    neg_a = -jnp.exp(alog_ref[...])
    dtb = dtb_ref[...]
    normw = normw_ref[...]
    nlev = max(1, int(math.ceil(math.log2(c))))
    rowmask = None
    if valid < tile:
        rowmask = jnp.where(lax.broadcasted_iota(jnp.int32, (c, 1), 0) < valid, 1.0, 0.0)
    heads = range(GDN_HEADS)

    blocks = [(bb, ic) for bb in seqs for ic in range(nc)]
    beta_blks, cums, cum_ts = {}, {}, {}
    for bb, ic in blocks:
        abc = ab_ref[bb, ic * c:(ic + 1) * c, :]
        beta_blk = jax.nn.sigmoid(abc)
        xg = abc + dtb
        g_blk = neg_a * (jnp.maximum(xg, 0.0) + jnp.log1p(jnp.exp(-jnp.abs(xg))))
        if rowmask is not None:
            beta_blk = beta_blk * rowmask
            g_blk = g_blk * rowmask
        g1, g2, g3 = _split3(g_blk)
        beta_blks[bb, ic] = beta_blk
        cums[bb, ic] = _nn(tri, g1) + _nn(tri, g2) + _nn(tri, g3)
        cum_ts[bb, ic] = cums[bb, ic].T

    chains = [(bb, ic, h) for bb, ic in blocks for h in heads]
    qs, ks, vs, betas, gcs, decs, kks, qks = {}, {}, {}, {}, {}, {}, {}, {}
    for key in chains:
        bb, ic, h = key
        rows = slice(ic * c, (ic + 1) * c)
        q = act_ref[bb, rows, h * GDN_DK:(h + 1) * GDN_DK]
        k = act_ref[bb, rows, GDN_QK + h * GDN_DK:GDN_QK + (h + 1) * GDN_DK]
        v = act_ref[bb, rows, 2 * GDN_QK + h * GDN_DV:2 * GDN_QK + (h + 1) * GDN_DV]
        q = q * lax.rsqrt(jnp.sum(q * q, axis=-1, keepdims=True) + RMS_EPS) * (GDN_DK ** -0.5)
        k = k * lax.rsqrt(jnp.sum(k * k, axis=-1, keepdims=True) + RMS_EPS)
        if rowmask is not None:
            k = k * rowmask
            v = v * rowmask
        gc = cums[bb, ic][:, GDN_HEADS + h:GDN_HEADS + h + 1]
        gr = cum_ts[bb, ic][GDN_HEADS + h:GDN_HEADS + h + 1, :]
        k1 = k.astype(BF16)
        qs[key], ks[key], vs[key], gcs[key] = q, k, v, gc
        betas[key] = beta_blks[bb, ic][:, h:h + 1]
        decs[key] = jnp.where(incl, jnp.exp(jnp.where(incl, gc - gr, 0.0)), 0.0)
        kks[key] = _nt(k1, k1)
        qks[key] = _nt(q.astype(BF16), k1)
    xs = {key: -jnp.where(strict, betas[key] * kks[key] * decs[key], 0.0) for key in chains}
    ams = dict(xs)
    for _ in range(nlev - 1):
        xbs = {key: xs[key].astype(BF16) for key in chains}
        xs = {key: _nn(xbs[key], xbs[key]) for key in chains}
        ams = {key: ams[key] + xs[key] + _nn(ams[key].astype(BF16), xs[key].astype(BF16)) for key in chains}
    egs = {key: jnp.exp(gcs[key]) for key in chains}
    u_bars, wq, qkb, g_lasts, k_decs = {}, {}, {}, {}, {}
    for key in chains:
        rhs = jnp.concatenate([betas[key] * vs[key], (betas[key] * egs[key]) * ks[key]], axis=-1)
        sol = rhs + _nn(ams[key].astype(BF16), rhs.astype(BF16))
        u_bars[key] = sol[:, :GDN_DV]
        wq[key] = jnp.concatenate([sol[:, GDN_DV:], egs[key] * qs[key]], axis=0).astype(BF16)
        qkb[key] = (qks[key] * decs[key]).astype(BF16)
        g_lasts[key] = gcs[key][c - 1:c, :]
        k_decs[key] = (jnp.exp(g_lasts[key] - gcs[key]) * ks[key]).astype(BF16)

    lanes = [(bb, h) for bb in seqs for h in heads]
    states = {(bb, h): state_ref[bb, h] for bb, h in lanes}
    for ic in range(nc):
        rows = slice(ic * c, (ic + 1) * c)
        sbs = {ln: states[ln].astype(BF16) for ln in lanes}
        ws = {(bb, h): _nn(wq[bb, ic, h], sbs[bb, h]) for bb, h in lanes}
        ubs = {(bb, h): (u_bars[bb, ic, h] - ws[bb, h][:c]).astype(BF16) for bb, h in lanes}
        o2 = {(bb, h): _nn(qkb[bb, ic, h], ubs[bb, h]) for bb, h in lanes}
        ds = {(bb, h): _tn(k_decs[bb, ic, h], ubs[bb, h]) for bb, h in lanes}
        for bb, h in lanes:
            states[bb, h] = jnp.exp(g_lasts[bb, ic, h]) * states[bb, h] + ds[bb, h]
            o = ws[bb, h][c:] + o2[bb, h]
            zz = z_ref[bb, rows, h * GDN_DV:(h + 1) * GDN_DV]
            ya_ref[bb, rows, h * GDN_DV:(h + 1) * GDN_DV] = _rms_norm(o, normw) * _silu(zz)
    for bb, h in lanes:
        state_ref[bb, h] = states[bb, h]

    @pl.when(t == nt - 1)
    def _():
        sout_ref[...] = state_ref[...]


def _recurrent(src, w_rec, s0, gbuf0, sbuf0, convw, alog_row, dtb_row, normw, scw, *, nb, tile, chunk, valid):
    b, l, _ = src.shape
    assert b % nb == 0 and l % tile == 0
    nt = l // tile
    project = w_rec is not None

    def col(width, start):
        return pl.BlockSpec((nb, tile, width), lambda i, t: (i, t, start // width))

    def per_b(shape):
        return pl.BlockSpec((nb,) + shape, lambda i, t: (i,) + (0,) * len(shape))

    if project:
        src_specs = [pl.BlockSpec((nb, tile, D_MODEL), lambda i, t: (i, t, 0)), _const_spec((D_MODEL, REC_COLS))]
        src_args = [src, w_rec]
        extra_scratch = [pltpu.VMEM((nb, tile, GDN_V), F32), pltpu.VMEM((nb, tile, SC_WIDTH), F32),
                         pltpu.VMEM((nb, tile, LANE), F32)]
    else:
        src_specs = [col(GDN_CONV_CH, COL_QKV), col(GDN_V, COL_Z), col(SC_WIDTH, COL_SB), col(SC_WIDTH, COL_SC),
                     col(SC_WIDTH, COL_SH), col(LANE, COL_AB)]
        src_args = [src] * 6
        extra_scratch = []
    pad = SUBLANE
    kern = functools.partial(_recurrent_kernel, nb=nb, tile=tile, chunk=chunk, valid=valid, project=project)
    return pl.pallas_call(
        kern,
        grid=(b // nb, nt),
        in_specs=src_specs + [
            per_b((GDN_HEADS, GDN_DK, GDN_DV)), per_b((GDN_CONV - 1, GDN_CONV_CH)), per_b((SC_CONV - 1, SC_WIDTH)),
            _const_spec((GDN_CONV, GDN_CONV_CH)), _const_spec((1, LANE)), _const_spec((1, LANE)),
            _const_spec((1, GDN_DV)), _const_spec((SC_CONV, SC_WIDTH))],
        out_specs=[pl.BlockSpec((nb, tile, GDN_V), lambda i, t: (i, t, 0)),
                   pl.BlockSpec((nb, tile, SC_WIDTH), lambda i, t: (i, t, 0)),
                   per_b((GDN_HEADS, GDN_DK, GDN_DV)), per_b((GDN_CONV - 1, GDN_CONV_CH)),
                   per_b((SC_CONV - 1, SC_WIDTH))],
        out_shape=[jax.ShapeDtypeStruct((b, l, GDN_V), F32), jax.ShapeDtypeStruct((b, l, SC_WIDTH), F32),
                   jax.ShapeDtypeStruct((b, GDN_HEADS, GDN_DK, GDN_DV), F32),
                   jax.ShapeDtypeStruct((b, GDN_CONV - 1, GDN_CONV_CH), F32),
                   jax.ShapeDtypeStruct((b, SC_CONV - 1, SC_WIDTH), F32)],
        scratch_shapes=[pltpu.VMEM((nb, tile + pad, GDN_CONV_CH), F32), pltpu.VMEM((nb, tile + pad, SC_WIDTH), F32),
                        pltpu.VMEM((nb, tile, GDN_CONV_CH), F32), pltpu.VMEM((nb, GDN_HEADS, GDN_DK, GDN_DV), F32)]
                       + extra_scratch,
        compiler_params=pltpu.CompilerParams(
            dimension_semantics=("parallel", "arbitrary"), vmem_limit_bytes=VMEM_LIMIT),
        name="recurrent",
    )(*src_args, s0, gbuf0, sbuf0, convw, alog_row, dtb_row, normw, scw)


def _diff_lambda(lam_ref, lam_init):
    lq = lam_ref[...]
    a = jnp.sum(lq[0:1, :] * lq[1:2, :], axis=-1, keepdims=True)
    b = jnp.sum(lq[2:3, :] * lq[3:4, :], axis=-1, keepdims=True)
    return jnp.exp(a) - jnp.exp(b) + lam_init


def _softmax_parts(s_off, s_diag):
    m = jnp.max(s_diag, axis=-1, keepdims=True)
    if s_off is not None:
        m = jnp.maximum(m, jnp.max(s_off, axis=-1, keepdims=True))
    e_diag = jnp.exp2(s_diag - m)
    total = jnp.sum(e_diag, axis=-1, keepdims=True)
    e_off = None
    if s_off is not None:
        e_off = jnp.exp2(s_off - m)
        total = total + jnp.sum(e_off, axis=-1, keepdims=True)
    return e_off, e_diag, total


def _attn_prompt_kernel(*refs, tq, nt, lam_init, aliased):
    x_ref, w_ref, lam_ref, nw_ref = refs[:4]
    refs = refs[4 + (2 if aliased else 0):]
    yc_ref, krow_ref, vrow_ref, q_s, k_s, v_s, o_s = refs
    t = pl.program_id(1)
    xb = x_ref[...].astype(BF16)
    q = _nn(xb, w_ref[:, 0:DIFF_QK])
    k = _nn(xb, w_ref[:, DIFF_QK:2 * DIFF_QK])
    v = _nn(xb, w_ref[:, 2 * DIFF_QK:ATT_COLS])
    r0 = pl.multiple_of(t * tq, tq)
    lane = lax.broadcasted_iota(jnp.int32, (1, 2 * DIFF_HD), 1)
    for h in range(DIFF_HEADS):
        qh = q[:, h * LANE:(h + 1) * LANE]
        kh = k[:, h * LANE:(h + 1) * LANE]
        vh = v[:, h * LANE:(h + 1) * LANE]
        krow_ref[pl.ds(h, tq, stride=DIFF_HEADS), :] = kh
        vrow_ref[pl.ds(h, tq, stride=DIFF_HEADS), :] = vh
        q_s[h, 0:tq, :] = jnp.where(lane < DIFF_HD, qh, 0.0).astype(BF16)
        q_s[h, tq:2 * tq, :] = jnp.where(lane >= DIFF_HD, qh, 0.0).astype(BF16)
        k_s[h, pl.ds(r0, tq), :] = kh.astype(BF16)
        v_s[h, pl.ds(r0, tq), :] = vh.astype(BF16)
    lam = _diff_lambda(lam_ref, lam_init)
    nw = nw_ref[...]
    row = lax.broadcasted_iota(jnp.int32, (2 * tq, tq), 0)
    causal = lax.broadcasted_iota(jnp.int32, (2 * tq, tq), 1) <= jnp.where(row < tq, row, row - tq)
    sc = DIFF_SCALE * math.log2(math.e)
    group = 2

    for i in range(nt):
        off = i * tq

        @pl.when(t == i)
        def _(off=off):
            def group_body(g, carry):
                hs = [g * group + u for u in range(group)]
                qs = [q_s[h] for h in hs]
                s_diag = [jnp.where(causal, _nt(qs[u], k_s[h, off:off + tq, :]) * sc, NEG_BIG)
                          for u, h in enumerate(hs)]
                s_off = [_nt(qs[u], k_s[h, 0:off, :]) * sc if off else None for u, h in enumerate(hs)]
                parts = [_softmax_parts(so, sd) for so, sd in zip(s_off, s_diag)]
                outs = [_nn(parts[u][1].astype(BF16), v_s[h, off:off + tq, :]) for u, h in enumerate(hs)]
                if off:
                    outs = [outs[u] + _nn(parts[u][0].astype(BF16), v_s[h, 0:off, :]) for u, h in enumerate(hs)]
                for u, h in enumerate(hs):
                    on = outs[u] * (1.0 / parts[u][2])
                    o = on[0:tq] - lam * on[tq:2 * tq]
                    o_s[h] = _rms_norm(o, nw) * (1.0 - lam_init)
                return carry

            lax.fori_loop(0, DIFF_HEADS // group, group_body, 0)

    for h in range(DIFF_HEADS):
        yc_ref[:, h * DIFF_VD:(h + 1) * DIFF_VD] = o_s[h]


def _attn_prompt(x3, w_att, lam_p, nw, lam_init, tq, layer, depth, kv_prev):
    b, l, _ = x3.shape
    nt = l // tq
    aliased = kv_prev is not None
    kern = functools.partial(_attn_prompt_kernel, tq=tq, nt=nt, lam_init=lam_init, aliased=aliased)
    row = lambda w: pl.BlockSpec((None, tq, w), lambda i, t: (i, t, 0))
    kv_row = pl.BlockSpec((None, None, tq * DIFF_HEADS, DIFF_VD), lambda i, t: (i, layer, t, 0))
    kv_shape = jax.ShapeDtypeStruct((b, depth, l * DIFF_HEADS, DIFF_VD), F32)
    in_specs = [row(D_MODEL), _const_spec((D_MODEL, ATT_COLS)), _const_spec((4, DIFF_HD)), _const_spec((1, DIFF_VD))]
    args = [x3, w_att, lam_p, nw]
    aliases = {}
    if aliased:
        in_specs += [pl.BlockSpec(memory_space=pl.ANY)] * 2
        args += list(kv_prev)
        aliases = {4: 1, 5: 2}
    return pl.pallas_call(
        kern,
        grid=(b, nt),
        in_specs=in_specs,
        out_specs=[row(DIFF_V), kv_row, kv_row],
        out_shape=[jax.ShapeDtypeStruct((b, l, DIFF_V), F32), kv_shape, kv_shape],
        scratch_shapes=[pltpu.VMEM((DIFF_HEADS, 2 * tq, 2 * DIFF_HD), BF16),
                        pltpu.VMEM((DIFF_HEADS, l, 2 * DIFF_HD), BF16), pltpu.VMEM((DIFF_HEADS, l, DIFF_VD), BF16),
                        pltpu.VMEM((DIFF_HEADS, tq, DIFF_VD), F32)],
        input_output_aliases=aliases,
        compiler_params=pltpu.CompilerParams(
            dimension_semantics=("parallel", "arbitrary"), vmem_limit_bytes=VMEM_LIMIT),
        name="attn_prompt",
    )(*args)


def _attn_sample_kernel(pt_ref, q_ref, kn_ref, vn_ref, lam_ref, nw_ref, *rest, pages, lam_init):
    k_refs = rest[:pages]
    v_refs = rest[pages:2 * pages]
    o_ref = rest[2 * pages]
    m_ref, l_ref, acc_ref = rest[2 * pages + 1:]
    j = pl.program_id(1)
    nj = pl.num_programs(1)
    rows = DIFF_HEADS * 2 * 4
    rpp = PAGE_SIZE * DIFF_HEADS

    @pl.when(j == 0)
    def _():
        m_ref[...] = jnp.full((rows, 1), NEG_BIG, F32)
        l_ref[...] = jnp.zeros((rows, 1), F32)
        acc_ref[...] = jnp.zeros((rows, DIFF_VD), F32)

    q = q_ref[...]

    def update(s, pv):
        m_old = m_ref[...]
        m_new = jnp.maximum(m_old, jnp.max(s, axis=-1, keepdims=True))
        alpha = jnp.exp(m_old - m_new)
        p = jnp.exp(s - m_new)
        l_ref[...] = alpha * l_ref[...] + jnp.sum(p, axis=-1, keepdims=True)
        acc_ref[...] = alpha * acc_ref[...] + pv(p)
        m_ref[...] = m_new

    def head_rows(page_refs, h):
        return jnp.concatenate([r[pl.ds(h, PAGE_SIZE, stride=DIFF_HEADS), :].astype(BF16) for r in page_refs], axis=0)

    rph = rows // DIFF_HEADS
    s = jnp.concatenate([_nt(q[h * rph:(h + 1) * rph].astype(BF16), head_rows(k_refs, h))
                         for h in range(DIFF_HEADS)], axis=0) * DIFF_SCALE

    def pv_past(p):
        return jnp.concatenate([_nn(p[h * rph:(h + 1) * rph].astype(BF16), head_rows(v_refs, h))
                                for h in range(DIFF_HEADS)], axis=0)

    update(s, pv_past)

    @pl.when(j == nj - 1)
    def _():
        nk = kn_ref.shape[0]
        r = lax.broadcasted_iota(jnp.int32, (rows, nk), 0)
        cc = lax.broadcasted_iota(jnp.int32, (rows, nk), 1)
        ok = ((cc & (DIFF_HEADS - 1)) == (r >> 3)) & ((cc >> 2) <= (r & 3))
        sn = jnp.where(ok, _nt(q.astype(BF16), kn_ref[...].astype(BF16)) * DIFF_SCALE, NEG_BIG)
        vnb = vn_ref[...].astype(BF16)
        update(sn, lambda p: _nn(p.astype(BF16), vnb))
        lam = _diff_lambda(lam_ref, lam_init)
        o = acc_ref[...] / l_ref[...]
        nw = nw_ref[...]
        for h in range(DIFF_HEADS):
            blk = o[h * 8:(h + 1) * 8, :]
            d = blk - lam * pltpu.roll(blk, 4, 0)
            o_ref[h] = _rms_norm(d, nw) * (1.0 - lam_init)


def _attn_sample(q_rows, k_new, v_new, lam_p, nw, cache_k4, cache_v4, page_table, layer, lam_init, pages):
    b = q_rows.shape[0]
    n_pages = page_table.shape[1]
    rows = q_rows.shape[1]
    rpp = PAGE_SIZE * DIFF_HEADS

    def page_spec(i):
        return pl.BlockSpec((None, None, rpp, DIFF_VD), lambda bi, j, pt: (pt[bi, j * pages + i], layer, 0, 0))

    def per_b(shape):
        return pl.BlockSpec((None,) + shape, lambda bi, j, pt: (bi,) + (0,) * len(shape))

    kern = functools.partial(_attn_sample_kernel, pages=pages, lam_init=lam_init)
    grid_spec = pltpu.PrefetchScalarGridSpec(
        num_scalar_prefetch=1,
        grid=(b, n_pages // pages),
        in_specs=[per_b((rows, 2 * DIFF_HD)), per_b(k_new.shape[1:]), per_b(v_new.shape[1:]),
                  pl.BlockSpec((4, DIFF_HD), lambda bi, j, pt: (0, 0)),
                  pl.BlockSpec((1, DIFF_VD), lambda bi, j, pt: (0, 0))]
                 + [page_spec(i) for i in range(pages)] + [page_spec(i) for i in range(pages)],
        out_specs=per_b((DIFF_HEADS, 8, DIFF_VD)),
        scratch_shapes=[pltpu.VMEM((rows, 1), F32), pltpu.VMEM((rows, 1), F32), pltpu.VMEM((rows, DIFF_VD), F32)],
    )
    return pl.pallas_call(
        kern,
        grid_spec=grid_spec,
        out_shape=jax.ShapeDtypeStruct((b, DIFF_HEADS, 8, DIFF_VD), F32),
        compiler_params=pltpu.CompilerParams(
            dimension_semantics=("parallel", "arbitrary"), vmem_limit_bytes=VMEM_LIMIT),
        name="attn_sample",
    )(page_table, q_rows, k_new, v_new, lam_p, nw, *([cache_k4] * pages), *([cache_v4] * pages))


def _merge_mlp_kernel(x_ref, ya_ref, yb_ref, yc_ref, wg_ref, wb_ref, wo_ref, l1g_ref, l1b_ref,
                      wu_ref, wd_ref, l2g_ref, l2b_ref, o_ref, *, alpha, ff_chunk):
    x = x_ref[...]
    xb = x.astype(BF16)
    merged = None
    for i, y_ref in enumerate((ya_ref, yb_ref, yc_ref)):
        gate = jax.nn.sigmoid(_nn(xb, wg_ref[:, i * D_MODEL:(i + 1) * D_MODEL]))
        term = gate * _nn(y_ref[...].astype(BF16), wb_ref[i])
        merged = term if merged is None else merged + term
    mix = _nn(merged.astype(BF16), wo_ref[...])
    x1 = _layer_norm(alpha * x + mix, l1g_ref[...], l1b_ref[...])
    x1b = x1.astype(BF16)
    acc = None
    for cidx in range(D_FF // ff_chunk):
        up = jnp.maximum(_nn(x1b, wu_ref[:, cidx * ff_chunk:(cidx + 1) * ff_chunk]), 0.0)
        term = _nn((up * up).astype(BF16), wd_ref[cidx * ff_chunk:(cidx + 1) * ff_chunk, :])
        acc = term if acc is None else acc + term
    o_ref[...] = _layer_norm(alpha * x1 + acc, l2g_ref[...], l2b_ref[...])


def _merge_mlp(x2, ya2, yb2, yc2, wl, tm, alpha):
    n = x2.shape[0]
    row = lambda w: pl.BlockSpec((tm, w), lambda i: (i, 0))
    return pl.pallas_call(
        functools.partial(_merge_mlp_kernel, alpha=alpha, ff_chunk=1024),
        grid=(n // tm,),
        in_specs=[row(D_MODEL), row(BRANCH_W), row(BRANCH_W), row(BRANCH_W),
                  _const_spec((D_MODEL, N_BRANCH * D_MODEL)), _const_spec((N_BRANCH, BRANCH_W, D_MODEL)),
                  _const_spec((D_MODEL, D_MODEL)), _const_spec((1, D_MODEL)), _const_spec((1, D_MODEL)),
                  _const_spec((D_MODEL, D_FF)), _const_spec((D_FF, D_MODEL)),
                  _const_spec((1, D_MODEL)), _const_spec((1, D_MODEL))],
        out_specs=row(D_MODEL),
        out_shape=jax.ShapeDtypeStruct((n, D_MODEL), F32),
        compiler_params=pltpu.CompilerParams(dimension_semantics=("parallel",), vmem_limit_bytes=VMEM_LIMIT),
        name="merge_mlp",
    )(x2, ya2, yb2, yc2, wl["w_gates"], wl["w_branch"], wl["w_o"], wl["ln1_g"], wl["ln1_b"],
      wl["w_up"], wl["w_down"], wl["ln2_g"], wl["ln2_b"])


def _regroup_w_in(w):
    o_z = GDN_CONV_CH
    o_ab = o_z + GDN_V
    o_sb = o_ab + 2 * GDN_HEADS
    o_dq = o_sb + 3 * SC_WIDTH
    o_gates = o_dq + 2 * DIFF_QK + DIFF_V
    wb = w.astype(BF16)
    zeros = jnp.zeros((w.shape[0], LANE - 2 * GDN_HEADS), BF16)
    w_rec = jnp.concatenate([wb[:, :o_ab], wb[:, o_sb:o_dq], wb[:, o_ab:o_sb], zeros], axis=1)
    return w_rec, wb[:, o_dq:o_gates], wb[:, o_gates:]


def _lane_row(vals, offset):
    n = vals.shape[0]
    return jnp.pad(vals.astype(F32), (offset, LANE - offset - n)).reshape(1, LANE)


def kernel(x_prompt, x_sample, cache_k, cache_v, page_table, state_gdn, state_gdn_conv, state_sc_conv, w_in,
           gdn_conv_w, gdn_a_log, gdn_dt_bias, gdn_norm_w, sc_conv_w, diff_lambda, diff_norm_w, w_branch, w_o,
           ln1_g, ln1_b, ln2_g, ln2_b, w_up, w_down):
    depth = w_in.shape[0]
    alpha = (2 * depth) ** 0.25
    b_p, seq, _ = x_prompt.shape
    b_s, dec_seq, _ = x_sample.shape
    n_phys = cache_k.shape[0]
    assert dec_seq == 4 and seq % 256 == 0
    dec_pad = SUBLANE
    rpp = PAGE_SIZE * DIFF_HEADS
    cache_k4 = cache_k.reshape(n_phys, depth, rpp, 2 * DIFF_HD)
    cache_v4 = cache_v.reshape(n_phys, depth, rpp, DIFF_VD)
    prompt_tile = 256
    tok_tile = 512
    sample_pages = 32
    assert page_table.shape[1] % sample_pages == 0

    xp = x_prompt
    xs = jnp.pad(x_sample, ((0, 0), (0, dec_pad - dec_seq), (0, 0)))
    half = (jnp.arange(2 * DIFF_HD) < DIFF_HD)
    map_mask = jnp.stack([half, ~half]).astype(F32)
    rows_p = [[], [], [], [], []]
    rows_s = [[], [], [], [], []]
    kv_prompt = None
    for l in range(depth):
        lam_init = 0.8 - 0.6 * math.exp(-0.3 * l)
        w_rec, w_att, w_gates = _regroup_w_in(w_in[l])
        wl = {
            "w_gates": w_gates,
            "w_branch": w_branch[l].astype(BF16), "w_o": w_o[l].astype(BF16),
            "ln1_g": ln1_g[l].reshape(1, D_MODEL), "ln1_b": ln1_b[l].reshape(1, D_MODEL),
            "ln2_g": ln2_g[l].reshape(1, D_MODEL), "ln2_b": ln2_b[l].reshape(1, D_MODEL),
            "w_up": w_up[l].astype(BF16), "w_down": w_down[l].astype(BF16),
        }
        rec_w = (gdn_conv_w[l], _lane_row(gdn_a_log[l], GDN_HEADS), _lane_row(gdn_dt_bias[l], GDN_HEADS),
                 gdn_norm_w[l].reshape(1, GDN_DV), sc_conv_w[l])
        lam_p = diff_lambda[l].astype(F32)
        nw = diff_norm_w[l].reshape(1, DIFF_VD)

        ya, yb, s_p, gbuf_p, sbuf_p = _recurrent(
            xp, w_rec, jnp.zeros((b_p, GDN_HEADS, GDN_DK, GDN_DV), F32),
            jnp.zeros((b_p, GDN_CONV - 1, GDN_CONV_CH), F32), jnp.zeros((b_p, SC_CONV - 1, SC_WIDTH), F32),
            *rec_w, nb=2, tile=prompt_tile, chunk=GDN_CHUNK, valid=prompt_tile)
        yc, *kv_prompt = _attn_prompt(xp, w_att, lam_p, nw, lam_init, prompt_tile, l, depth, kv_prompt)
        n_p = b_p * seq
        xp = _merge_mlp(xp.reshape(n_p, D_MODEL), ya.reshape(n_p, GDN_V), yb.reshape(n_p, SC_WIDTH),
                        yc.reshape(n_p, DIFF_V), wl, tok_tile, alpha).reshape(b_p, seq, D_MODEL)
        rows_p[2].append(s_p)
        rows_p[3].append(gbuf_p)
        rows_p[4].append(sbuf_p)

        n_s = b_s * dec_pad
        h_rec, h_att = _in_proj(xs.reshape(n_s, D_MODEL), w_rec, w_att)
        h_rec = h_rec.reshape(b_s, dec_pad, REC_COLS)
        h_att = h_att.reshape(b_s, dec_pad, ATT_COLS)
        ya, yb, s_s, gbuf_s, sbuf_s = _recurrent(
            h_rec, None, state_gdn[:, l], state_gdn_conv[:, l], state_sc_conv[:, l],
            *rec_w, nb=8, tile=dec_pad, chunk=dec_pad, valid=dec_seq)
        q = h_att[:, :dec_seq, 0:DIFF_QK].reshape(b_s, dec_seq, DIFF_HEADS, 2 * DIFF_HD)
        q = jnp.transpose(q, (0, 2, 1, 3))[:, :, None] * map_mask[None, None, :, None, :]
        q_rows = q.reshape(b_s, DIFF_HEADS * 2 * dec_seq, 2 * DIFF_HD)
        k_new = h_att[:, :dec_seq, DIFF_QK:2 * DIFF_QK].reshape(b_s, dec_seq * DIFF_HEADS, 2 * DIFF_HD)
        v_new = h_att[:, :dec_seq, 2 * DIFF_QK:ATT_COLS].reshape(b_s, dec_seq * DIFF_HEADS, DIFF_VD)
        o = _attn_sample(q_rows, jnp.pad(k_new, ((0, 0), (0, LANE - dec_seq * DIFF_HEADS), (0, 0))),
                         jnp.pad(v_new, ((0, 0), (0, LANE - dec_seq * DIFF_HEADS), (0, 0))),
                         lam_p, nw, cache_k4, cache_v4, page_table, l, lam_init, pages=sample_pages)
        yc = jnp.transpose(o[:, :, :dec_seq, :], (0, 2, 1, 3)).reshape(b_s, dec_seq, DIFF_V)
        yc = jnp.pad(yc, ((0, 0), (0, dec_pad - dec_seq), (0, 0)))
        xs = _merge_mlp(xs.reshape(n_s, D_MODEL), ya.reshape(n_s, GDN_V), yb.reshape(n_s, SC_WIDTH),
                        yc.reshape(n_s, DIFF_V), wl, n_s, alpha).reshape(b_s, dec_pad, D_MODEL)
        rows_s[0].append(k_new.reshape(b_s, dec_seq, DIFF_HEADS, 2 * DIFF_HD))
        rows_s[1].append(v_new.reshape(b_s, dec_seq, DIFF_HEADS, DIFF_VD))
        rows_s[2].append(s_s)
        rows_s[3].append(gbuf_s)
        rows_s[4].append(sbuf_s)

    outs_p = [a.reshape(b_p, depth, seq, DIFF_HEADS, DIFF_VD) for a in kv_prompt]
    outs_p += [jnp.stack(r, axis=1) for r in rows_p[2:]]
    outs_s = [jnp.stack(r, axis=1) for r in rows_s]
    return (xp, xs[:, :dec_seq], *outs_p, *outs_s)
```

```python
import functools
import math

import jax
import jax.numpy as jnp
from jax import lax
from jax.experimental import pallas as pl
from jax.experimental.pallas import tpu as pltpu

F32 = jnp.float32
BF16 = jnp.bfloat16

D_MODEL = 1024
GDN_HEADS = 4
GDN_DK = 128
GDN_DV = 128
GDN_CONV = 4
GDN_QK = GDN_HEADS * GDN_DK
GDN_V = GDN_HEADS * GDN_DV
GDN_CONV_CH = 2 * GDN_QK + GDN_V
GDN_CHUNK = 64
SC_WIDTH = 512
SC_CONV = 3
DIFF_HEADS = 4
DIFF_HD = 64
DIFF_VD = 128
DIFF_QK = DIFF_HEADS * 2 * DIFF_HD
DIFF_V = DIFF_HEADS * DIFF_VD
DIFF_SCALE = DIFF_HD ** -0.5
PAGE_SIZE = 128
N_BRANCH = 3
BRANCH_W = 512
D_FF = 4 * D_MODEL
LN_EPS = 1e-5
RMS_EPS = 1e-6
NEG_BIG = -1e30

LANE = 128
SUBLANE = 8

COL_QKV = 0
COL_Z = COL_QKV + GDN_CONV_CH
COL_SB = COL_Z + GDN_V
COL_SC = COL_SB + SC_WIDTH
COL_SH = COL_SC + SC_WIDTH
COL_AB = COL_SH + SC_WIDTH
REC_COLS = COL_AB + LANE
ATT_COLS = DIFF_QK + DIFF_QK + DIFF_V
GATE_COLS = N_BRANCH * D_MODEL
W_REC_AT = 0
W_ATT_AT = 3 * ATT_COLS
W_GATES_AT = 2 * GATE_COLS
W_ALL_COLS = W_GATES_AT + GATE_COLS

VMEM_LIMIT = 56 * 1024 * 1024


def _nt(a, b):
    return lax.dot_general(a, b, (((1,), (1,)), ((), ())), preferred_element_type=F32)


def _tn(a, b):
    return lax.dot_general(a, b, (((0,), (0,)), ((), ())), preferred_element_type=F32)


def _nn(a, b):
    return jnp.dot(a, b, preferred_element_type=F32)


def _split3(x):
    x1 = x.astype(BF16)
    r1 = x - x1.astype(F32)
    x2 = r1.astype(BF16)
    r2 = r1 - x2.astype(F32)
    return x1, x2, r2.astype(BF16)


def _silu(x):
    return x * jax.nn.sigmoid(x)


def _layer_norm(x, g, b):
    mu = jnp.mean(x, axis=-1, keepdims=True)
    xc = x - mu
    var = jnp.mean(xc * xc, axis=-1, keepdims=True)
    return xc * lax.rsqrt(var + LN_EPS) * g + b


def _rms_norm(x, w):
    return x * lax.rsqrt(jnp.mean(x * x, axis=-1, keepdims=True) + RMS_EPS) * w


def _const_spec(shape):
    return pl.BlockSpec(shape, lambda *_: (0,) * len(shape), pipeline_mode=pl.Buffered(1))


def _weight_cols_spec(width, start):
    assert start % width == 0
    return pl.BlockSpec((D_MODEL, width), lambda *_: (0, start // width), pipeline_mode=pl.Buffered(1))


def _in_proj_kernel(x_ref, wr_ref, wa_ref, hr_ref, ha_ref):
    xb = x_ref[...].astype(BF16)
    hr_ref[...] = _nn(xb, wr_ref[...])
    ha_ref[...] = _nn(xb, wa_ref[...])


def _in_proj(x2, w_all):
    n = x2.shape[0]
    full = lambda shape: pl.BlockSpec(shape, lambda i: (0, 0))
    return pl.pallas_call(
        _in_proj_kernel,
        grid=(1,),
        in_specs=[full((n, D_MODEL)), _weight_cols_spec(REC_COLS, W_REC_AT), _weight_cols_spec(ATT_COLS, W_ATT_AT)],
        out_specs=[full((n, REC_COLS)), full((n, ATT_COLS))],
        out_shape=[jax.ShapeDtypeStruct((n, REC_COLS), F32), jax.ShapeDtypeStruct((n, ATT_COLS), F32)],
        compiler_params=pltpu.CompilerParams(dimension_semantics=("arbitrary",), vmem_limit_bytes=VMEM_LIMIT),
        name="in_proj",
    )(x2, w_all, w_all)


def _recurrent_kernel(*refs, nb, tile, chunk, valid, project):
    if project:
        x_ref, w_ref = refs[:2]
        refs = refs[2:]
    else:
        qkv_ref, z_ref, sb_ref, sc_ref, sh_ref, ab_ref = refs[:6]
        refs = refs[6:]
    (s0_ref, gbuf0_ref, sbuf0_ref, convw_ref, alog_ref, dtb_ref, normw_ref, scw_ref,
     ya_ref, yb_ref, sout_ref, gbuf_out_ref, sbuf_out_ref,
     ext_ref, scext_ref, act_ref, state_ref) = refs[:17]
    t = pl.program_id(1)
    nt = pl.num_programs(1)
    pad = SUBLANE
    seqs = range(nb)

    @pl.when(t == 0)
    def _():
        state_ref[...] = s0_ref[...]
        for bb in seqs:
            ext_ref[bb, 0:pad, :] = jnp.zeros((pad, GDN_CONV_CH), F32)
            ext_ref[bb, pad - (GDN_CONV - 1):pad, :] = gbuf0_ref[bb]
            scext_ref[bb, 0:pad, :] = jnp.zeros((pad, SC_WIDTH), F32)
            scext_ref[bb, pad - (SC_CONV - 1):pad, :] = sbuf0_ref[bb]

    if project:
        z_ref, sb_ref, ab_ref = refs[17:20]
    else:
        ext_ref[:, pad:pad + tile, :] = qkv_ref[...]
        scext_ref[:, pad:pad + tile, :] = sc_ref[...] * sh_ref[...]

    rb = min(tile, 64)
    convw = convw_ref[...]
    scw = scw_ref[...]
    for bb in seqs:
        if project:
            xb = x_ref[bb].astype(BF16)
            ext_ref[bb, pad:pad + tile, :] = _nn(xb, w_ref[:, COL_QKV:COL_Z])
            z_ref[bb] = _nn(xb, w_ref[:, COL_Z:COL_SB])
            sb_ref[bb] = _nn(xb, w_ref[:, COL_SB:COL_SC])
            scext_ref[bb, pad:pad + tile, :] = _nn(xb, w_ref[:, COL_SC:COL_SH]) * _nn(xb, w_ref[:, COL_SH:COL_AB])
            ab_ref[bb] = _nn(xb, w_ref[:, COL_AB:REC_COLS])
        for r in range(tile // rb):
            base = pad + r * rb
            acc = ext_ref[bb, base - 3:base - 3 + rb, :] * convw[0:1, :]
            for j in range(1, GDN_CONV):
                acc = acc + ext_ref[bb, base - 3 + j:base - 3 + j + rb, :] * convw[j:j + 1, :]
            act_ref[bb, r * rb:(r + 1) * rb, :] = _silu(acc)
            u = scext_ref[bb, base - 2:base - 2 + rb, :] * scw[0:1, :]
            for j in range(1, SC_CONV):
                u = u + scext_ref[bb, base - 2 + j:base - 2 + j + rb, :] * scw[j:j + 1, :]
            yb_ref[bb, r * rb:(r + 1) * rb, :] = sb_ref[bb, r * rb:(r + 1) * rb, :] * u

    @pl.when(t == nt - 1)
    def _():
        gbuf_out_ref[...] = ext_ref[:, pad + valid - (GDN_CONV - 1):pad + valid, :]
        sbuf_out_ref[...] = scext_ref[:, pad + valid - (SC_CONV - 1):pad + valid, :]

    ext_ref[:, 0:pad, :] = ext_ref[:, tile:tile + pad, :]
    scext_ref[:, 0:pad, :] = scext_ref[:, tile:tile + pad, :]

    c = chunk
    nc = tile // c
    ri = lax.broadcasted_iota(jnp.int32, (c, c), 0)
    ci = lax.broadcasted_iota(jnp.int32, (c, c), 1)
    incl = ri >= ci
    strict = ri > ci
    tri = jnp.where(incl, 1.0, 0.0).astype(BF16)
    neg_a = -jnp.exp(alog_ref[...])
    dtb = dtb_ref[...]
    normw = normw_ref[...]
    nlev = max(1, int(math.ceil(math.log2(c))))
    rowmask = None
    if valid < tile:
        rowmask = jnp.where(lax.broadcasted_iota(jnp.int32, (c, 1), 0) < valid, 1.0, 0.0)
    heads = range(GDN_HEADS)

    blocks = [(bb, ic) for bb in seqs for ic in range(nc)]
    beta_blks, cums, cum_ts = {}, {}, {}
    for bb, ic in blocks:
        abc = ab_ref[bb, ic * c:(ic + 1) * c, :]
        beta_blk = jax.nn.sigmoid(abc)
        xg = abc + dtb
        g_blk = neg_a * (jnp.maximum(xg, 0.0) + jnp.log1p(jnp.exp(-jnp.abs(xg))))
        if rowmask is not None:
            beta_blk = beta_blk * rowmask
            g_blk = g_blk * rowmask
        g1, g2, g3 = _split3(g_blk)
        beta_blks[bb, ic] = beta_blk
        cums[bb, ic] = _nn(tri, g1) + _nn(tri, g2) + _nn(tri, g3)
        cum_ts[bb, ic] = cums[bb, ic].T

    chains = [(bb, ic, h) for bb, ic in blocks for h in heads]
    qs, ks, vs, betas, gcs, decs, kks, qks = {}, {}, {}, {}, {}, {}, {}, {}
    for key in chains:
        bb, ic, h = key
        rows = slice(ic * c, (ic + 1) * c)
        q = act_ref[bb, rows, h * GDN_DK:(h + 1) * GDN_DK]
        k = act_ref[bb, rows, GDN_QK + h * GDN_DK:GDN_QK + (h + 1) * GDN_DK]
        v = act_ref[bb, rows, 2 * GDN_QK + h * GDN_DV:2 * GDN_QK + (h + 1) * GDN_DV]
        q = q * lax.rsqrt(jnp.sum(q * q, axis=-1, keepdims=True) + RMS_EPS) * (GDN_DK ** -0.5)
        k = k * lax.rsqrt(jnp.sum(k * k, axis=-1, keepdims=True) + RMS_EPS)
        if rowmask is not None:
            k = k * rowmask
            v = v * rowmask
        gc = cums[bb, ic][:, GDN_HEADS + h:GDN_HEADS + h + 1]
        gr = cum_ts[bb, ic][GDN_HEADS + h:GDN_HEADS + h + 1, :]
        k1 = k.astype(BF16)
        qs[key], ks[key], vs[key], gcs[key] = q, k, v, gc
        betas[key] = beta_blks[bb, ic][:, h:h + 1]
        decs[key] = jnp.where(incl, jnp.exp(jnp.where(incl, gc - gr, 0.0)), 0.0)
        kks[key] = _nt(k1, k1)
        qks[key] = _nt(q.astype(BF16), k1)
    xs = {key: -jnp.where(strict, betas[key] * kks[key] * decs[key], 0.0) for key in chains}
    ams = dict(xs)
    for _ in range(nlev - 1):
        xbs = {key: xs[key].astype(BF16) for key in chains}
        xs = {key: _nn(xbs[key], xbs[key]) for key in chains}
        ams = {key: ams[key] + xs[key] + _nn(ams[key].astype(BF16), xs[key].astype(BF16)) for key in chains}
    egs = {key: jnp.exp(gcs[key]) for key in chains}
    u_bars, wq, qkb, g_lasts, k_decs = {}, {}, {}, {}, {}
    for key in chains:
        rhs = jnp.concatenate([betas[key] * vs[key], (betas[key] * egs[key]) * ks[key]], axis=-1)
        sol = rhs + _nn(ams[key].astype(BF16), rhs.astype(BF16))
        u_bars[key] = sol[:, :GDN_DV]
        wq[key] = jnp.concatenate([sol[:, GDN_DV:], egs[key] * qs[key]], axis=0).astype(BF16)
        qkb[key] = (qks[key] * decs[key]).astype(BF16)
        g_lasts[key] = gcs[key][c - 1:c, :]
        k_decs[key] = (jnp.exp(g_lasts[key] - gcs[key]) * ks[key]).astype(BF16)

    lanes = [(bb, h) for bb in seqs for h in heads]
    states = {(bb, h): state_ref[bb, h] for bb, h in lanes}
    for ic in range(nc):
        rows = slice(ic * c, (ic + 1) * c)
        sbs = {ln: states[ln].astype(BF16) for ln in lanes}
        ws = {(bb, h): _nn(wq[bb, ic, h], sbs[bb, h]) for bb, h in lanes}
        ubs = {(bb, h): (u_bars[bb, ic, h] - ws[bb, h][:c]).astype(BF16) for bb, h in lanes}
        o2 = {(bb, h): _nn(qkb[bb, ic, h], ubs[bb, h]) for bb, h in lanes}
        ds = {(bb, h): _tn(k_decs[bb, ic, h], ubs[bb, h]) for bb, h in lanes}
        for bb, h in lanes:
            states[bb, h] = jnp.exp(g_lasts[bb, ic, h]) * states[bb, h] + ds[bb, h]
            o = ws[bb, h][c:] + o2[bb, h]
            zz = z_ref[bb, rows, h * GDN_DV:(h + 1) * GDN_DV]
            ya_ref[bb, rows, h * GDN_DV:(h + 1) * GDN_DV] = _rms_norm(o, normw) * _silu(zz)
    for bb, h in lanes:
        state_ref[bb, h] = states[bb, h]

    @pl.when(t == nt - 1)
    def _():
        sout_ref[...] = state_ref[...]


def _recurrent(src, w_all, s0, gbuf0, sbuf0, convw, alog_row, dtb_row, normw, scw, *, nb, tile, chunk, valid):
    b, l, _ = src.shape
    assert b % nb == 0 and l % tile == 0
    nt = l // tile
    project = w_all is not None

    def col(width, start):
        return pl.BlockSpec((nb, tile, width), lambda i, t: (i, t, start // width))

    def per_b(shape):
        return pl.BlockSpec((nb,) + shape, lambda i, t: (i,) + (0,) * len(shape))

    if project:
        src_specs = [pl.BlockSpec((nb, tile, D_MODEL), lambda i, t: (i, t, 0)), _weight_cols_spec(REC_COLS, W_REC_AT)]
        src_args = [src, w_all]
        extra_scratch = [pltpu.VMEM((nb, tile, GDN_V), F32), pltpu.VMEM((nb, tile, SC_WIDTH), F32),
                         pltpu.VMEM((nb, tile, LANE), F32)]
    else:
        src_specs = [col(GDN_CONV_CH, COL_QKV), col(GDN_V, COL_Z), col(SC_WIDTH, COL_SB), col(SC_WIDTH, COL_SC),
                     col(SC_WIDTH, COL_SH), col(LANE, COL_AB)]
        src_args = [src] * 6
        extra_scratch = []
    pad = SUBLANE
    kern = functools.partial(_recurrent_kernel, nb=nb, tile=tile, chunk=chunk, valid=valid, project=project)
    return pl.pallas_call(
        kern,
        grid=(b // nb, nt),
        in_specs=src_specs + [
            per_b((GDN_HEADS, GDN_DK, GDN_DV)), per_b((GDN_CONV - 1, GDN_CONV_CH)), per_b((SC_CONV - 1, SC_WIDTH)),
            _const_spec((GDN_CONV, GDN_CONV_CH)), _const_spec((1, LANE)), _const_spec((1, LANE)),
            _const_spec((1, GDN_DV)), _const_spec((SC_CONV, SC_WIDTH))],
        out_specs=[pl.BlockSpec((nb, tile, GDN_V), lambda i, t: (i, t, 0)),
                   pl.BlockSpec((nb, tile, SC_WIDTH), lambda i, t: (i, t, 0)),
                   per_b((GDN_HEADS, GDN_DK, GDN_DV)), per_b((GDN_CONV - 1, GDN_CONV_CH)),
                   per_b((SC_CONV - 1, SC_WIDTH))],
        out_shape=[jax.ShapeDtypeStruct((b, l, GDN_V), F32), jax.ShapeDtypeStruct((b, l, SC_WIDTH), F32),
                   jax.ShapeDtypeStruct((b, GDN_HEADS, GDN_DK, GDN_DV), F32),
                   jax.ShapeDtypeStruct((b, GDN_CONV - 1, GDN_CONV_CH), F32),
                   jax.ShapeDtypeStruct((b, SC_CONV - 1, SC_WIDTH), F32)],
        scratch_shapes=[pltpu.VMEM((nb, tile + pad, GDN_CONV_CH), F32), pltpu.VMEM((nb, tile + pad, SC_WIDTH), F32),
                        pltpu.VMEM((nb, tile, GDN_CONV_CH), F32), pltpu.VMEM((nb, GDN_HEADS, GDN_DK, GDN_DV), F32)]
                       + extra_scratch,
        compiler_params=pltpu.CompilerParams(
            dimension_semantics=("parallel", "arbitrary"), vmem_limit_bytes=VMEM_LIMIT),
        name="recurrent",
    )(*src_args, s0, gbuf0, sbuf0, convw, alog_row, dtb_row, normw, scw)


def _diff_lambda(lam_ref, lam_init):
    lq = lam_ref[...]
    a = jnp.sum(lq[0:1, :] * lq[1:2, :], axis=-1, keepdims=True)
    b = jnp.sum(lq[2:3, :] * lq[3:4, :], axis=-1, keepdims=True)
    return jnp.exp(a) - jnp.exp(b) + lam_init


def _softmax_parts(s_off, s_diag):
    m = jnp.max(s_diag, axis=-1, keepdims=True)
    if s_off is not None:
        m = jnp.maximum(m, jnp.max(s_off, axis=-1, keepdims=True))
    e_diag = jnp.exp2(s_diag - m)
    total = jnp.sum(e_diag, axis=-1, keepdims=True)
    e_off = None
    if s_off is not None:
        e_off = jnp.exp2(s_off - m)
        total = total + jnp.sum(e_off, axis=-1, keepdims=True)
    return e_off, e_diag, total


def _attn_prompt_kernel(*refs, tq, nt, lam_init, aliased):
    x_ref, w_ref, lam_ref, nw_ref = refs[:4]
    refs = refs[4 + (2 if aliased else 0):]
    yc_ref, krow_ref, vrow_ref, q_s, k_s, v_s, o_s = refs
    t = pl.program_id(1)
    xb = x_ref[...].astype(BF16)
    q = _nn(xb, w_ref[:, 0:DIFF_QK])
    k = _nn(xb, w_ref[:, DIFF_QK:2 * DIFF_QK])
    v = _nn(xb, w_ref[:, 2 * DIFF_QK:ATT_COLS])
    r0 = pl.multiple_of(t * tq, tq)
    lane = lax.broadcasted_iota(jnp.int32, (1, 2 * DIFF_HD), 1)
    for h in range(DIFF_HEADS):
        qh = q[:, h * LANE:(h + 1) * LANE]
        kh = k[:, h * LANE:(h + 1) * LANE]
        vh = v[:, h * LANE:(h + 1) * LANE]
        krow_ref[pl.ds(h, tq, stride=DIFF_HEADS), :] = kh
        vrow_ref[pl.ds(h, tq, stride=DIFF_HEADS), :] = vh
        q_s[h, 0:tq, :] = jnp.where(lane < DIFF_HD, qh, 0.0).astype(BF16)
        q_s[h, tq:2 * tq, :] = jnp.where(lane >= DIFF_HD, qh, 0.0).astype(BF16)
        k_s[h, pl.ds(r0, tq), :] = kh.astype(BF16)
        v_s[h, pl.ds(r0, tq), :] = vh.astype(BF16)
    lam = _diff_lambda(lam_ref, lam_init)
    nw = nw_ref[...]
    row = lax.broadcasted_iota(jnp.int32, (2 * tq, tq), 0)
    causal = lax.broadcasted_iota(jnp.int32, (2 * tq, tq), 1) <= jnp.where(row < tq, row, row - tq)
    sc = DIFF_SCALE * math.log2(math.e)
    group = 2

    for i in range(nt):
        off = i * tq

        @pl.when(t == i)
        def _(off=off):
            def group_body(g, carry):
                hs = [g * group + u for u in range(group)]
                qs = [q_s[h] for h in hs]
                s_diag = [jnp.where(causal, _nt(qs[u], k_s[h, off:off + tq, :]) * sc, NEG_BIG)
                          for u, h in enumerate(hs)]
                s_off = [_nt(qs[u], k_s[h, 0:off, :]) * sc if off else None for u, h in enumerate(hs)]
                parts = [_softmax_parts(so, sd) for so, sd in zip(s_off, s_diag)]
                outs = [_nn(parts[u][1].astype(BF16), v_s[h, off:off + tq, :]) for u, h in enumerate(hs)]
                if off:
                    outs = [outs[u] + _nn(parts[u][0].astype(BF16), v_s[h, 0:off, :]) for u, h in enumerate(hs)]
                for u, h in enumerate(hs):
                    on = outs[u] * (1.0 / parts[u][2])
                    o = on[0:tq] - lam * on[tq:2 * tq]
                    o_s[h] = _rms_norm(o, nw) * (1.0 - lam_init)
                return carry

            lax.fori_loop(0, DIFF_HEADS // group, group_body, 0)

    for h in range(DIFF_HEADS):
        yc_ref[:, h * DIFF_VD:(h + 1) * DIFF_VD] = o_s[h]


def _attn_prompt(x3, w_all, lam_p, nw, lam_init, tq, layer, depth, kv_prev):
    b, l, _ = x3.shape
    nt = l // tq
    aliased = kv_prev is not None
    kern = functools.partial(_attn_prompt_kernel, tq=tq, nt=nt, lam_init=lam_init, aliased=aliased)
    row = lambda w: pl.BlockSpec((None, tq, w), lambda i, t: (i, t, 0))
    kv_row = pl.BlockSpec((None, None, tq * DIFF_HEADS, DIFF_VD), lambda i, t: (i, layer, t, 0))
    kv_shape = jax.ShapeDtypeStruct((b, depth, l * DIFF_HEADS, DIFF_VD), F32)
    in_specs = [row(D_MODEL), _weight_cols_spec(ATT_COLS, W_ATT_AT), _const_spec((4, DIFF_HD)),
                _const_spec((1, DIFF_VD))]
    args = [x3, w_all, lam_p, nw]
    aliases = {}
    if aliased:
        in_specs += [pl.BlockSpec(memory_space=pl.ANY)] * 2
        args += list(kv_prev)
        aliases = {4: 1, 5: 2}
    return pl.pallas_call(
        kern,
        grid=(b, nt),
        in_specs=in_specs,
        out_specs=[row(DIFF_V), kv_row, kv_row],
        out_shape=[jax.ShapeDtypeStruct((b, l, DIFF_V), F32), kv_shape, kv_shape],
        scratch_shapes=[pltpu.VMEM((DIFF_HEADS, 2 * tq, 2 * DIFF_HD), BF16),
                        pltpu.VMEM((DIFF_HEADS, l, 2 * DIFF_HD), BF16), pltpu.VMEM((DIFF_HEADS, l, DIFF_VD), BF16),
                        pltpu.VMEM((DIFF_HEADS, tq, DIFF_VD), F32)],
        input_output_aliases=aliases,
        compiler_params=pltpu.CompilerParams(
            dimension_semantics=("parallel", "arbitrary"), vmem_limit_bytes=VMEM_LIMIT),
        name="attn_prompt",
    )(*args)


def _attn_sample_kernel(pt_ref, q_ref, kn_ref, vn_ref, lam_ref, nw_ref, *rest, pages, lam_init):
    k_refs = rest[:pages]
    v_refs = rest[pages:2 * pages]
    o_ref = rest[2 * pages]
    m_ref, l_ref, acc_ref = rest[2 * pages + 1:]
    j = pl.program_id(1)
    nj = pl.num_programs(1)
    rows = DIFF_HEADS * 2 * 4
    rpp = PAGE_SIZE * DIFF_HEADS

    @pl.when(j == 0)
    def _():
        m_ref[...] = jnp.full((rows, 1), NEG_BIG, F32)
        l_ref[...] = jnp.zeros((rows, 1), F32)
        acc_ref[...] = jnp.zeros((rows, DIFF_VD), F32)

    q = q_ref[...]

    def update(s, pv):
        m_old = m_ref[...]
        m_new = jnp.maximum(m_old, jnp.max(s, axis=-1, keepdims=True))
        alpha = jnp.exp(m_old - m_new)
        p = jnp.exp(s - m_new)
        l_ref[...] = alpha * l_ref[...] + jnp.sum(p, axis=-1, keepdims=True)
        acc_ref[...] = alpha * acc_ref[...] + pv(p)
        m_ref[...] = m_new

    def head_rows(page_refs, h):
        return jnp.concatenate([r[pl.ds(h, PAGE_SIZE, stride=DIFF_HEADS), :].astype(BF16) for r in page_refs], axis=0)

    rph = rows // DIFF_HEADS
    s = jnp.concatenate([_nt(q[h * rph:(h + 1) * rph].astype(BF16), head_rows(k_refs, h))
                         for h in range(DIFF_HEADS)], axis=0) * DIFF_SCALE

    def pv_past(p):
        return jnp.concatenate([_nn(p[h * rph:(h + 1) * rph].astype(BF16), head_rows(v_refs, h))
                                for h in range(DIFF_HEADS)], axis=0)

    update(s, pv_past)

    @pl.when(j == nj - 1)
    def _():
        nk = kn_ref.shape[0]
        r = lax.broadcasted_iota(jnp.int32, (rows, nk), 0)
        cc = lax.broadcasted_iota(jnp.int32, (rows, nk), 1)
        ok = ((cc & (DIFF_HEADS - 1)) == (r >> 3)) & ((cc >> 2) <= (r & 3))
        sn = jnp.where(ok, _nt(q.astype(BF16), kn_ref[...].astype(BF16)) * DIFF_SCALE, NEG_BIG)
        vnb = vn_ref[...].astype(BF16)
        update(sn, lambda p: _nn(p.astype(BF16), vnb))
        lam = _diff_lambda(lam_ref, lam_init)
        o = acc_ref[...] / l_ref[...]
        nw = nw_ref[...]
        for h in range(DIFF_HEADS):
            blk = o[h * 8:(h + 1) * 8, :]
            d = blk - lam * pltpu.roll(blk, 4, 0)
            o_ref[h] = _rms_norm(d, nw) * (1.0 - lam_init)


def _attn_sample(q_rows, k_new, v_new, lam_p, nw, cache_k4, cache_v4, page_table, layer, lam_init, pages):
    b = q_rows.shape[0]
    n_pages = page_table.shape[1]
    rows = q_rows.shape[1]
    rpp = PAGE_SIZE * DIFF_HEADS

    def page_spec(i):
        return pl.BlockSpec((None, None, rpp, DIFF_VD), lambda bi, j, pt: (pt[bi, j * pages + i], layer, 0, 0))

    def per_b(shape):
        return pl.BlockSpec((None,) + shape, lambda bi, j, pt: (bi,) + (0,) * len(shape))

    kern = functools.partial(_attn_sample_kernel, pages=pages, lam_init=lam_init)
    grid_spec = pltpu.PrefetchScalarGridSpec(
        num_scalar_prefetch=1,
        grid=(b, n_pages // pages),
        in_specs=[per_b((rows, 2 * DIFF_HD)), per_b(k_new.shape[1:]), per_b(v_new.shape[1:]),
                  pl.BlockSpec((4, DIFF_HD), lambda bi, j, pt: (0, 0)),
                  pl.BlockSpec((1, DIFF_VD), lambda bi, j, pt: (0, 0))]
                 + [page_spec(i) for i in range(pages)] + [page_spec(i) for i in range(pages)],
        out_specs=per_b((DIFF_HEADS, 8, DIFF_VD)),
        scratch_shapes=[pltpu.VMEM((rows, 1), F32), pltpu.VMEM((rows, 1), F32), pltpu.VMEM((rows, DIFF_VD), F32)],
    )
    return pl.pallas_call(
        kern,
        grid_spec=grid_spec,
        out_shape=jax.ShapeDtypeStruct((b, DIFF_HEADS, 8, DIFF_VD), F32),
        compiler_params=pltpu.CompilerParams(
            dimension_semantics=("parallel", "arbitrary"), vmem_limit_bytes=VMEM_LIMIT),
        name="attn_sample",
    )(page_table, q_rows, k_new, v_new, lam_p, nw, *([cache_k4] * pages), *([cache_v4] * pages))


def _merge_mlp_kernel(x_ref, ya_ref, yb_ref, yc_ref, wg_ref, wb_ref, wo_ref, l1g_ref, l1b_ref,
                      wu_ref, wd_ref, l2g_ref, l2b_ref, o_ref, *, alpha, ff_chunk):
    x = x_ref[...]
    xb = x.astype(BF16)
    merged = None
    for i, y_ref in enumerate((ya_ref, yb_ref, yc_ref)):
        gate = jax.nn.sigmoid(_nn(xb, wg_ref[:, i * D_MODEL:(i + 1) * D_MODEL]))
        term = gate * _nn(y_ref[...].astype(BF16), wb_ref[i])
        merged = term if merged is None else merged + term
    mix = _nn(merged.astype(BF16), wo_ref[...])
    x1 = _layer_norm(alpha * x + mix, l1g_ref[...], l1b_ref[...])
    x1b = x1.astype(BF16)
    acc = None
    for cidx in range(D_FF // ff_chunk):
        up = jnp.maximum(_nn(x1b, wu_ref[:, cidx * ff_chunk:(cidx + 1) * ff_chunk]), 0.0)
        term = _nn((up * up).astype(BF16), wd_ref[cidx * ff_chunk:(cidx + 1) * ff_chunk, :])
        acc = term if acc is None else acc + term
    o_ref[...] = _layer_norm(alpha * x1 + acc, l2g_ref[...], l2b_ref[...])


def _merge_mlp(x2, ya2, yb2, yc2, wl, tm, alpha):
    n = x2.shape[0]
    row = lambda w: pl.BlockSpec((tm, w), lambda i: (i, 0))
    return pl.pallas_call(
        functools.partial(_merge_mlp_kernel, alpha=alpha, ff_chunk=1024),
        grid=(n // tm,),
        in_specs=[row(D_MODEL), row(BRANCH_W), row(BRANCH_W), row(BRANCH_W),
                  _weight_cols_spec(GATE_COLS, W_GATES_AT), _const_spec((N_BRANCH, BRANCH_W, D_MODEL)),
                  _const_spec((D_MODEL, D_MODEL)), _const_spec((1, D_MODEL)), _const_spec((1, D_MODEL)),
                  _const_spec((D_MODEL, D_FF)), _const_spec((D_FF, D_MODEL)),
                  _const_spec((1, D_MODEL)), _const_spec((1, D_MODEL))],
        out_specs=row(D_MODEL),
        out_shape=jax.ShapeDtypeStruct((n, D_MODEL), F32),
        compiler_params=pltpu.CompilerParams(dimension_semantics=("parallel",), vmem_limit_bytes=VMEM_LIMIT),
        name="merge_mlp",
    )(x2, ya2, yb2, yc2, wl["w_all"], wl["w_branch"], wl["w_o"], wl["ln1_g"], wl["ln1_b"],
      wl["w_up"], wl["w_down"], wl["ln2_g"], wl["ln2_b"])


def _regroup_w_in(w):
    o_ab = GDN_CONV_CH + GDN_V
    o_sb = o_ab + 2 * GDN_HEADS
    o_dq = o_sb + 3 * SC_WIDTH
    zeros = jnp.zeros((w.shape[0], W_ATT_AT - REC_COLS + LANE - 2 * GDN_HEADS), w.dtype)
    w_all = jnp.concatenate([w[:, :o_ab], w[:, o_sb:o_dq], w[:, o_ab:o_sb], zeros, w[:, o_dq:]], axis=1)
    assert w_all.shape[1] == W_ALL_COLS
    return w_all.astype(BF16)


def _lane_row(vals, offset):
    n = vals.shape[0]
    return jnp.pad(vals.astype(F32), (offset, LANE - offset - n)).reshape(1, LANE)


def kernel(x_prompt, x_sample, cache_k, cache_v, page_table, state_gdn, state_gdn_conv, state_sc_conv, w_in,
           gdn_conv_w, gdn_a_log, gdn_dt_bias, gdn_norm_w, sc_conv_w, diff_lambda, diff_norm_w, w_branch, w_o,
           ln1_g, ln1_b, ln2_g, ln2_b, w_up, w_down):
    depth = w_in.shape[0]
    alpha = (2 * depth) ** 0.25
    b_p, seq, _ = x_prompt.shape
    b_s, dec_seq, _ = x_sample.shape
    n_phys = cache_k.shape[0]
    assert dec_seq == 4 and seq % 256 == 0
    dec_pad = SUBLANE
    rpp = PAGE_SIZE * DIFF_HEADS
    cache_k4 = cache_k.reshape(n_phys, depth, rpp, 2 * DIFF_HD)
    cache_v4 = cache_v.reshape(n_phys, depth, rpp, DIFF_VD)
    prompt_tile = 256
    tok_tile = 512
    sample_pages = 32
    assert page_table.shape[1] % sample_pages == 0

    xp = x_prompt
    xs = jnp.pad(x_sample, ((0, 0), (0, dec_pad - dec_seq), (0, 0)))
    half = (jnp.arange(2 * DIFF_HD) < DIFF_HD)
    map_mask = jnp.stack([half, ~half]).astype(F32)
    rows_p = [[], [], [], [], []]
    rows_s = [[], [], [], [], []]
    kv_prompt = None
    for l in range(depth):
        lam_init = 0.8 - 0.6 * math.exp(-0.3 * l)
        w_all = _regroup_w_in(w_in[l])
        wl = {
            "w_all": w_all,
            "w_branch": w_branch[l].astype(BF16), "w_o": w_o[l].astype(BF16),
            "ln1_g": ln1_g[l].reshape(1, D_MODEL), "ln1_b": ln1_b[l].reshape(1, D_MODEL),
            "ln2_g": ln2_g[l].reshape(1, D_MODEL), "ln2_b": ln2_b[l].reshape(1, D_MODEL),
            "w_up": w_up[l].astype(BF16), "w_down": w_down[l].astype(BF16),
        }
        rec_w = (gdn_conv_w[l], _lane_row(gdn_a_log[l], GDN_HEADS), _lane_row(gdn_dt_bias[l], GDN_HEADS),
                 gdn_norm_w[l].reshape(1, GDN_DV), sc_conv_w[l])
        lam_p = diff_lambda[l].astype(F32)
        nw = diff_norm_w[l].reshape(1, DIFF_VD)

        ya, yb, s_p, gbuf_p, sbuf_p = _recurrent(
            xp, w_all, jnp.zeros((b_p, GDN_HEADS, GDN_DK, GDN_DV), F32),
            jnp.zeros((b_p, GDN_CONV - 1, GDN_CONV_CH), F32), jnp.zeros((b_p, SC_CONV - 1, SC_WIDTH), F32),
            *rec_w, nb=2, tile=prompt_tile, chunk=GDN_CHUNK, valid=prompt_tile)
        yc, *kv_prompt = _attn_prompt(xp, w_all, lam_p, nw, lam_init, prompt_tile, l, depth, kv_prompt)
        n_p = b_p * seq
        xp = _merge_mlp(xp.reshape(n_p, D_MODEL), ya.reshape(n_p, GDN_V), yb.reshape(n_p, SC_WIDTH),
                        yc.reshape(n_p, DIFF_V), wl, tok_tile, alpha).reshape(b_p, seq, D_MODEL)
        rows_p[2].append(s_p)
        rows_p[3].append(gbuf_p)
        rows_p[4].append(sbuf_p)

        n_s = b_s * dec_pad
        h_rec, h_att = _in_proj(xs.reshape(n_s, D_MODEL), w_all)
        h_rec = h_rec.reshape(b_s, dec_pad, REC_COLS)
        h_att = h_att.reshape(b_s, dec_pad, ATT_COLS)
        ya, yb, s_s, gbuf_s, sbuf_s = _recurrent(
            h_rec, None, state_gdn[:, l], state_gdn_conv[:, l], state_sc_conv[:, l],
            *rec_w, nb=8, tile=dec_pad, chunk=dec_pad, valid=dec_seq)
        q = h_att[:, :dec_seq, 0:DIFF_QK].reshape(b_s, dec_seq, DIFF_HEADS, 2 * DIFF_HD)
        q = jnp.transpose(q, (0, 2, 1, 3))[:, :, None] * map_mask[None, None, :, None, :]
        q_rows = q.reshape(b_s, DIFF_HEADS * 2 * dec_seq, 2 * DIFF_HD)
        k_new = h_att[:, :dec_seq, DIFF_QK:2 * DIFF_QK].reshape(b_s, dec_seq * DIFF_HEADS, 2 * DIFF_HD)
        v_new = h_att[:, :dec_seq, 2 * DIFF_QK:ATT_COLS].reshape(b_s, dec_seq * DIFF_HEADS, DIFF_VD)
        o = _attn_sample(q_rows, jnp.pad(k_new, ((0, 0), (0, LANE - dec_seq * DIFF_HEADS), (0, 0))),
                         jnp.pad(v_new, ((0, 0), (0, LANE - dec_seq * DIFF_HEADS), (0, 0))),
                         lam_p, nw, cache_k4, cache_v4, page_table, l, lam_init, pages=sample_pages)
        yc = jnp.transpose(o[:, :, :dec_seq, :], (0, 2, 1, 3)).reshape(b_s, dec_seq, DIFF_V)
        yc = jnp.pad(yc, ((0, 0), (0, dec_pad - dec_seq), (0, 0)))
        xs = _merge_mlp(xs.reshape(n_s, D_MODEL), ya.reshape(n_s, GDN_V), yb.reshape(n_s, SC_WIDTH),
                        yc.reshape(n_s, DIFF_V), wl, n_s, alpha).reshape(b_s, dec_pad, D_MODEL)
        rows_s[0].append(k_new.reshape(b_s, dec_seq, DIFF_HEADS, 2 * DIFF_HD))
        rows_s[1].append(v_new.reshape(b_s, dec_seq, DIFF_HEADS, DIFF_VD))
        rows_s[2].append(s_s)
        rows_s[3].append(gbuf_s)
        rows_s[4].append(sbuf_s)

    outs_p = [a.reshape(b_p, depth, seq, DIFF_HEADS, DIFF_VD) for a in kv_prompt]
    outs_p += [jnp.stack(r, axis=1) for r in rows_p[2:]]
    outs_s = [jnp.stack(r, axis=1) for r in rows_s]
    return (xp, xs[:, :dec_seq], *outs_p, *outs_s)
```

```python
import functools
import math

import jax
import jax.numpy as jnp
from jax import lax
from jax.experimental import pallas as pl
from jax.experimental.pallas import tpu as pltpu

F32 = jnp.float32
BF16 = jnp.bfloat16

D_MODEL = 1024
GDN_HEADS = 4
GDN_DK = 128
GDN_DV = 128
GDN_CONV = 4
GDN_QK = GDN_HEADS * GDN_DK
GDN_V = GDN_HEADS * GDN_DV
GDN_CONV_CH = 2 * GDN_QK + GDN_V
GDN_CHUNK = 64
SC_WIDTH = 512
SC_CONV = 3
DIFF_HEADS = 4
DIFF_HD = 64
DIFF_VD = 128
DIFF_QK = DIFF_HEADS * 2 * DIFF_HD
DIFF_V = DIFF_HEADS * DIFF_VD
DIFF_SCALE = DIFF_HD ** -0.5
PAGE_SIZE = 128
N_BRANCH = 3
BRANCH_W = 512
D_FF = 4 * D_MODEL
LN_EPS = 1e-5
RMS_EPS = 1e-6
NEG_BIG = -1e30

LANE = 128
SUBLANE = 8

COL_QKV = 0
COL_Z = COL_QKV + GDN_CONV_CH
COL_SB = COL_Z + GDN_V
COL_SC = COL_SB + SC_WIDTH
COL_SH = COL_SC + SC_WIDTH
COL_AB = COL_SH + SC_WIDTH
REC_COLS = COL_AB + LANE
ATT_COLS = DIFF_QK + DIFF_QK + DIFF_V
GATE_COLS = N_BRANCH * D_MODEL
W_REC_AT = 0
W_ATT_AT = 3 * ATT_COLS
W_GATES_AT = 2 * GATE_COLS
W_ALL_COLS = W_GATES_AT + GATE_COLS

VMEM_LIMIT = 56 * 1024 * 1024


def _nt(a, b):
    return lax.dot_general(a, b, (((1,), (1,)), ((), ())), preferred_element_type=F32)


def _tn(a, b):
    return lax.dot_general(a, b, (((0,), (0,)), ((), ())), preferred_element_type=F32)


def _nn(a, b):
    return jnp.dot(a, b, preferred_element_type=F32)


def _split3(x):
    x1 = x.astype(BF16)
    r1 = x - x1.astype(F32)
    x2 = r1.astype(BF16)
    r2 = r1 - x2.astype(F32)
    return x1, x2, r2.astype(BF16)


def _silu(x):
    return x * jax.nn.sigmoid(x)


def _layer_norm(x, g, b):
    mu = jnp.mean(x, axis=-1, keepdims=True)
    xc = x - mu
    var = jnp.mean(xc * xc, axis=-1, keepdims=True)
    return xc * lax.rsqrt(var + LN_EPS) * g + b


def _rms_norm(x, w):
    return x * lax.rsqrt(jnp.mean(x * x, axis=-1, keepdims=True) + RMS_EPS) * w


def _const_spec(shape):
    return pl.BlockSpec(shape, lambda *_: (0,) * len(shape), pipeline_mode=pl.Buffered(1))


def _layer_spec(shape, layer):
    return pl.BlockSpec((None,) + shape, lambda *_: (layer,) + (0,) * len(shape), pipeline_mode=pl.Buffered(1))


def _weight_cols_spec(width, start, layer):
    assert start % width == 0
    return pl.BlockSpec((None, D_MODEL, width), lambda *_: (layer, 0, start // width), pipeline_mode=pl.Buffered(1))


def _in_proj_kernel(x_ref, wr_ref, wa_ref, hr_ref, ha_ref):
    xb = x_ref[...].astype(BF16)
    hr_ref[...] = _nn(xb, wr_ref[...])
    ha_ref[...] = _nn(xb, wa_ref[...])


def _in_proj(x2, w_all, layer):
    n = x2.shape[0]
    full = lambda shape: pl.BlockSpec(shape, lambda i: (0, 0))
    return pl.pallas_call(
        _in_proj_kernel,
        grid=(1,),
        in_specs=[full((n, D_MODEL)), _weight_cols_spec(REC_COLS, W_REC_AT, layer),
                  _weight_cols_spec(ATT_COLS, W_ATT_AT, layer)],
        out_specs=[full((n, REC_COLS)), full((n, ATT_COLS))],
        out_shape=[jax.ShapeDtypeStruct((n, REC_COLS), F32), jax.ShapeDtypeStruct((n, ATT_COLS), F32)],
        compiler_params=pltpu.CompilerParams(dimension_semantics=("arbitrary",), vmem_limit_bytes=VMEM_LIMIT),
        name="in_proj",
    )(x2, w_all, w_all)


def _recurrent_kernel(*refs, nb, tile, chunk, valid, project):
    if project:
        x_ref, w_ref = refs[:2]
        refs = refs[2:]
    else:
        qkv_ref, z_ref, sb_ref, sc_ref, sh_ref, ab_ref = refs[:6]
        refs = refs[6:]
    (s0_ref, gbuf0_ref, sbuf0_ref, convw_ref, alog_ref, dtb_ref, normw_ref, scw_ref,
     ya_ref, yb_ref, sout_ref, gbuf_out_ref, sbuf_out_ref,
     ext_ref, scext_ref, act_ref, state_ref) = refs[:17]
    t = pl.program_id(1)
    nt = pl.num_programs(1)
    pad = SUBLANE
    seqs = range(nb)

    @pl.when(t == 0)
    def _():
        state_ref[...] = s0_ref[...]
        for bb in seqs:
            ext_ref[bb, 0:pad, :] = jnp.zeros((pad, GDN_CONV_CH), F32)
            ext_ref[bb, pad - (GDN_CONV - 1):pad, :] = gbuf0_ref[bb]
            scext_ref[bb, 0:pad, :] = jnp.zeros((pad, SC_WIDTH), F32)
            scext_ref[bb, pad - (SC_CONV - 1):pad, :] = sbuf0_ref[bb]

    if project:
        z_ref, sb_ref, ab_ref = refs[17:20]
        xb = x_ref[...].reshape(nb * tile, D_MODEL).astype(BF16)

        def proj(lo, hi):
            return _nn(xb, w_ref[:, lo:hi]).reshape(nb, tile, hi - lo)

        ext_ref[:, pad:pad + tile, :] = proj(COL_QKV, COL_Z)
        z_ref[...] = proj(COL_Z, COL_SB)
        sb_ref[...] = proj(COL_SB, COL_SC)
        scext_ref[:, pad:pad + tile, :] = proj(COL_SC, COL_SH) * proj(COL_SH, COL_AB)
        ab_ref[...] = proj(COL_AB, REC_COLS)
    else:
        ext_ref[:, pad:pad + tile, :] = qkv_ref[...]
        scext_ref[:, pad:pad + tile, :] = sc_ref[...] * sh_ref[...]

    rb = min(tile, 64)
    convw = convw_ref[...]
    scw = scw_ref[...]
    for bb in seqs:
        for r in range(tile // rb):
            base = pad + r * rb
            acc = ext_ref[bb, base - 3:base - 3 + rb, :] * convw[0:1, :]
            for j in range(1, GDN_CONV):
                acc = acc + ext_ref[bb, base - 3 + j:base - 3 + j + rb, :] * convw[j:j + 1, :]
            act_ref[bb, r * rb:(r + 1) * rb, :] = _silu(acc)
            u = scext_ref[bb, base - 2:base - 2 + rb, :] * scw[0:1, :]
            for j in range(1, SC_CONV):
                u = u + scext_ref[bb, base - 2 + j:base - 2 + j + rb, :] * scw[j:j + 1, :]
            yb_ref[bb, r * rb:(r + 1) * rb, :] = sb_ref[bb, r * rb:(r + 1) * rb, :] * u

    @pl.when(t == nt - 1)
    def _():
        gbuf_out_ref[...] = ext_ref[:, pad + valid - (GDN_CONV - 1):pad + valid, :]
        sbuf_out_ref[...] = scext_ref[:, pad + valid - (SC_CONV - 1):pad + valid, :]

    ext_ref[:, 0:pad, :] = ext_ref[:, tile:tile + pad, :]
    scext_ref[:, 0:pad, :] = scext_ref[:, tile:tile + pad, :]

    c = chunk
    nc = tile // c
    ri = lax.broadcasted_iota(jnp.int32, (c, c), 0)
    ci = lax.broadcasted_iota(jnp.int32, (c, c), 1)
    incl = ri >= ci
    strict = ri > ci
    tri = jnp.where(incl, 1.0, 0.0).astype(BF16)
    neg_a = -jnp.exp(alog_ref[...])
    dtb = dtb_ref[...]
    normw = normw_ref[...]
    nlev = max(1, int(math.ceil(math.log2(c))))
    rowmask = None
    if valid < tile:
        rowmask = jnp.where(lax.broadcasted_iota(jnp.int32, (c, 1), 0) < valid, 1.0, 0.0)
    heads = range(GDN_HEADS)

    blocks = [(bb, ic) for bb in seqs for ic in range(nc)]
    beta_blks, cums, cum_ts = {}, {}, {}
    for bb, ic in blocks:
        abc = ab_ref[bb, ic * c:(ic + 1) * c, :]
        beta_blk = jax.nn.sigmoid(abc)
        xg = abc + dtb
        g_blk = neg_a * (jnp.maximum(xg, 0.0) + jnp.log1p(jnp.exp(-jnp.abs(xg))))
        if rowmask is not None:
            beta_blk = beta_blk * rowmask
            g_blk = g_blk * rowmask
        g1, g2, g3 = _split3(g_blk)
        beta_blks[bb, ic] = beta_blk
        cums[bb, ic] = _nn(tri, g1) + _nn(tri, g2) + _nn(tri, g3)
        cum_ts[bb, ic] = cums[bb, ic].T

    chains = [(bb, ic, h) for bb, ic in blocks for h in heads]
    qs, ks, vs, betas, gcs, decs, kks, qks = {}, {}, {}, {}, {}, {}, {}, {}
    for key in chains:
        bb, ic, h = key
        rows = slice(ic * c, (ic + 1) * c)
        q = act_ref[bb, rows, h * GDN_DK:(h + 1) * GDN_DK]
        k = act_ref[bb, rows, GDN_QK + h * GDN_DK:GDN_QK + (h + 1) * GDN_DK]
        v = act_ref[bb, rows, 2 * GDN_QK + h * GDN_DV:2 * GDN_QK + (h + 1) * GDN_DV]
        q = q * lax.rsqrt(jnp.sum(q * q, axis=-1, keepdims=True) + RMS_EPS) * (GDN_DK ** -0.5)
        k = k * lax.rsqrt(jnp.sum(k * k, axis=-1, keepdims=True) + RMS_EPS)
        if rowmask is not None:
            k = k * rowmask
            v = v * rowmask
        gc = cums[bb, ic][:, GDN_HEADS + h:GDN_HEADS + h + 1]
        gr = cum_ts[bb, ic][GDN_HEADS + h:GDN_HEADS + h + 1, :]
        k1 = k.astype(BF16)
        qs[key], ks[key], vs[key], gcs[key] = q, k, v, gc
        betas[key] = beta_blks[bb, ic][:, h:h + 1]
        decs[key] = jnp.where(incl, jnp.exp(jnp.where(incl, gc - gr, 0.0)), 0.0)
        kks[key] = _nt(k1, k1)
        qks[key] = _nt(q.astype(BF16), k1)
    xs = {key: -jnp.where(strict, betas[key] * kks[key] * decs[key], 0.0) for key in chains}
    ams = dict(xs)
    for _ in range(nlev - 1):
        xbs = {key: xs[key].astype(BF16) for key in chains}
        xs = {key: _nn(xbs[key], xbs[key]) for key in chains}
        ams = {key: ams[key] + xs[key] + _nn(ams[key].astype(BF16), xs[key].astype(BF16)) for key in chains}
    egs = {key: jnp.exp(gcs[key]) for key in chains}
    u_bars, wq, qkb, g_lasts, k_decs = {}, {}, {}, {}, {}
    for key in chains:
        rhs = jnp.concatenate([betas[key] * vs[key], (betas[key] * egs[key]) * ks[key]], axis=-1)
        sol = rhs + _nn(ams[key].astype(BF16), rhs.astype(BF16))
        u_bars[key] = sol[:, :GDN_DV]
        wq[key] = jnp.concatenate([sol[:, GDN_DV:], egs[key] * qs[key]], axis=0).astype(BF16)
        qkb[key] = (qks[key] * decs[key]).astype(BF16)
        g_lasts[key] = gcs[key][c - 1:c, :]
        k_decs[key] = (jnp.exp(g_lasts[key] - gcs[key]) * ks[key]).astype(BF16)

    lanes = [(bb, h) for bb in seqs for h in heads]
    states = {(bb, h): state_ref[bb, h] for bb, h in lanes}
    for ic in range(nc):
        rows = slice(ic * c, (ic + 1) * c)
        sbs = {ln: states[ln].astype(BF16) for ln in lanes}
        ws = {(bb, h): _nn(wq[bb, ic, h], sbs[bb, h]) for bb, h in lanes}
        ubs = {(bb, h): (u_bars[bb, ic, h] - ws[bb, h][:c]).astype(BF16) for bb, h in lanes}
        o2 = {(bb, h): _nn(qkb[bb, ic, h], ubs[bb, h]) for bb, h in lanes}
        ds = {(bb, h): _tn(k_decs[bb, ic, h], ubs[bb, h]) for bb, h in lanes}
        for bb, h in lanes:
            states[bb, h] = jnp.exp(g_lasts[bb, ic, h]) * states[bb, h] + ds[bb, h]
            o = ws[bb, h][c:] + o2[bb, h]
            zz = z_ref[bb, rows, h * GDN_DV:(h + 1) * GDN_DV]
            ya_ref[bb, rows, h * GDN_DV:(h + 1) * GDN_DV] = _rms_norm(o, normw) * _silu(zz)
    for bb, h in lanes:
        state_ref[bb, h] = states[bb, h]

    @pl.when(t == nt - 1)
    def _():
        sout_ref[...] = state_ref[...]


def _recurrent(src, w_all, s0, gbuf0, sbuf0, convw, alog_row, dtb_row, normw, scw, *, layer, nb, tile, chunk, valid):
    b, l, _ = src.shape
    assert b % nb == 0 and l % tile == 0
    nt = l // tile
    project = w_all is not None

    def col(width, start):
        return pl.BlockSpec((nb, tile, width), lambda i, t: (i, t, start // width))

    def per_b(shape):
        return pl.BlockSpec((nb,) + shape, lambda i, t: (i,) + (0,) * len(shape))

    if project:
        src_specs = [pl.BlockSpec((nb, tile, D_MODEL), lambda i, t: (i, t, 0)), _weight_cols_spec(REC_COLS, W_REC_AT, layer)]
        src_args = [src, w_all]
        extra_scratch = [pltpu.VMEM((nb, tile, GDN_V), F32), pltpu.VMEM((nb, tile, SC_WIDTH), F32),
                         pltpu.VMEM((nb, tile, LANE), F32)]
    else:
        src_specs = [col(GDN_CONV_CH, COL_QKV), col(GDN_V, COL_Z), col(SC_WIDTH, COL_SB), col(SC_WIDTH, COL_SC),
                     col(SC_WIDTH, COL_SH), col(LANE, COL_AB)]
        src_args = [src] * 6
        extra_scratch = []
    pad = SUBLANE
    kern = functools.partial(_recurrent_kernel, nb=nb, tile=tile, chunk=chunk, valid=valid, project=project)
    return pl.pallas_call(
        kern,
        grid=(b // nb, nt),
        in_specs=src_specs + [
            per_b((GDN_HEADS, GDN_DK, GDN_DV)), per_b((GDN_CONV - 1, GDN_CONV_CH)), per_b((SC_CONV - 1, SC_WIDTH)),
            _const_spec((GDN_CONV, GDN_CONV_CH)), _const_spec((1, LANE)), _const_spec((1, LANE)),
            _const_spec((1, GDN_DV)), _const_spec((SC_CONV, SC_WIDTH))],
        out_specs=[pl.BlockSpec((nb, tile, GDN_V), lambda i, t: (i, t, 0)),
                   pl.BlockSpec((nb, tile, SC_WIDTH), lambda i, t: (i, t, 0)),
                   per_b((GDN_HEADS, GDN_DK, GDN_DV)), per_b((GDN_CONV - 1, GDN_CONV_CH)),
                   per_b((SC_CONV - 1, SC_WIDTH))],
        out_shape=[jax.ShapeDtypeStruct((b, l, GDN_V), F32), jax.ShapeDtypeStruct((b, l, SC_WIDTH), F32),
                   jax.ShapeDtypeStruct((b, GDN_HEADS, GDN_DK, GDN_DV), F32),
                   jax.ShapeDtypeStruct((b, GDN_CONV - 1, GDN_CONV_CH), F32),
                   jax.ShapeDtypeStruct((b, SC_CONV - 1, SC_WIDTH), F32)],
        scratch_shapes=[pltpu.VMEM((nb, tile + pad, GDN_CONV_CH), F32), pltpu.VMEM((nb, tile + pad, SC_WIDTH), F32),
                        pltpu.VMEM((nb, tile, GDN_CONV_CH), F32), pltpu.VMEM((nb, GDN_HEADS, GDN_DK, GDN_DV), F32)]
                       + extra_scratch,
        compiler_params=pltpu.CompilerParams(
            dimension_semantics=("parallel", "arbitrary"), vmem_limit_bytes=VMEM_LIMIT),
        name="recurrent",
    )(*src_args, s0, gbuf0, sbuf0, convw, alog_row, dtb_row, normw, scw)


def _diff_lambda(lam_ref, lam_init):
    lq = lam_ref[...]
    a = jnp.sum(lq[0:1, :] * lq[1:2, :], axis=-1, keepdims=True)
    b = jnp.sum(lq[2:3, :] * lq[3:4, :], axis=-1, keepdims=True)
    return jnp.exp(a) - jnp.exp(b) + lam_init


def _softmax_parts(s_off, s_diag):
    m = jnp.max(s_diag, axis=-1, keepdims=True)
    if s_off is not None:
        m = jnp.maximum(m, jnp.max(s_off, axis=-1, keepdims=True))
    e_diag = jnp.exp2(s_diag - m)
    total = jnp.sum(e_diag, axis=-1, keepdims=True)
    e_off = None
    if s_off is not None:
        e_off = jnp.exp2(s_off - m)
        total = total + jnp.sum(e_off, axis=-1, keepdims=True)
    return e_off, e_diag, total


def _attn_prompt_kernel(*refs, tq, nt, lam_init, aliased):
    x_ref, w_ref, lam_ref, nw_ref = refs[:4]
    refs = refs[4 + (2 if aliased else 0):]
    yc_ref, krow_ref, vrow_ref, q_s, k_s, v_s, o_s = refs
    t = pl.program_id(1)
    xb = x_ref[...].astype(BF16)
    q = _nn(xb, w_ref[:, 0:DIFF_QK])
    k = _nn(xb, w_ref[:, DIFF_QK:2 * DIFF_QK])
    v = _nn(xb, w_ref[:, 2 * DIFF_QK:ATT_COLS])
    r0 = pl.multiple_of(t * tq, tq)
    lane = lax.broadcasted_iota(jnp.int32, (1, 2 * DIFF_HD), 1)
    for h in range(DIFF_HEADS):
        qh = q[:, h * LANE:(h + 1) * LANE]
        kh = k[:, h * LANE:(h + 1) * LANE]
        vh = v[:, h * LANE:(h + 1) * LANE]
        krow_ref[pl.ds(h, tq, stride=DIFF_HEADS), :] = kh
        vrow_ref[pl.ds(h, tq, stride=DIFF_HEADS), :] = vh
        q_s[h, 0:tq, :] = jnp.where(lane < DIFF_HD, qh, 0.0).astype(BF16)
        q_s[h, tq:2 * tq, :] = jnp.where(lane >= DIFF_HD, qh, 0.0).astype(BF16)
        k_s[h, pl.ds(r0, tq), :] = kh.astype(BF16)
        v_s[h, pl.ds(r0, tq), :] = vh.astype(BF16)
    lam = _diff_lambda(lam_ref, lam_init)
    nw = nw_ref[...]
    row = lax.broadcasted_iota(jnp.int32, (2 * tq, tq), 0)
    causal = lax.broadcasted_iota(jnp.int32, (2 * tq, tq), 1) <= jnp.where(row < tq, row, row - tq)
    sc = DIFF_SCALE * math.log2(math.e)
    group = 2

    for i in range(nt):
        off = i * tq

        @pl.when(t == i)
        def _(off=off):
            def group_body(g, carry):
                hs = [g * group + u for u in range(group)]
                qs = [q_s[h] for h in hs]
                s_diag = [jnp.where(causal, _nt(qs[u], k_s[h, off:off + tq, :]) * sc, NEG_BIG)
                          for u, h in enumerate(hs)]
                s_off = [_nt(qs[u], k_s[h, 0:off, :]) * sc if off else None for u, h in enumerate(hs)]
                parts = [_softmax_parts(so, sd) for so, sd in zip(s_off, s_diag)]
                outs = [_nn(parts[u][1].astype(BF16), v_s[h, off:off + tq, :]) for u, h in enumerate(hs)]
                if off:
                    outs = [outs[u] + _nn(parts[u][0].astype(BF16), v_s[h, 0:off, :]) for u, h in enumerate(hs)]
                for u, h in enumerate(hs):
                    on = outs[u] * (1.0 / parts[u][2])
                    o = on[0:tq] - lam * on[tq:2 * tq]
                    o_s[h] = _rms_norm(o, nw) * (1.0 - lam_init)
                return carry

            lax.fori_loop(0, DIFF_HEADS // group, group_body, 0)

    for h in range(DIFF_HEADS):
        yc_ref[:, h * DIFF_VD:(h + 1) * DIFF_VD] = o_s[h]


def _attn_prompt(x3, w_all, lam_p, nw, lam_init, tq, layer, depth, kv_prev):
    b, l, _ = x3.shape
    nt = l // tq
    aliased = kv_prev is not None
    kern = functools.partial(_attn_prompt_kernel, tq=tq, nt=nt, lam_init=lam_init, aliased=aliased)
    row = lambda w: pl.BlockSpec((None, tq, w), lambda i, t: (i, t, 0))
    kv_row = pl.BlockSpec((None, None, tq * DIFF_HEADS, DIFF_VD), lambda i, t: (i, layer, t, 0))
    kv_shape = jax.ShapeDtypeStruct((b, depth, l * DIFF_HEADS, DIFF_VD), F32)
    in_specs = [row(D_MODEL), _weight_cols_spec(ATT_COLS, W_ATT_AT, layer), _const_spec((4, DIFF_HD)),
                _const_spec((1, DIFF_VD))]
    args = [x3, w_all, lam_p, nw]
    aliases = {}
    if aliased:
        in_specs += [pl.BlockSpec(memory_space=pl.ANY)] * 2
        args += list(kv_prev)
        aliases = {4: 1, 5: 2}
    return pl.pallas_call(
        kern,
        grid=(b, nt),
        in_specs=in_specs,
        out_specs=[row(DIFF_V), kv_row, kv_row],
        out_shape=[jax.ShapeDtypeStruct((b, l, DIFF_V), F32), kv_shape, kv_shape],
        scratch_shapes=[pltpu.VMEM((DIFF_HEADS, 2 * tq, 2 * DIFF_HD), BF16),
                        pltpu.VMEM((DIFF_HEADS, l, 2 * DIFF_HD), BF16), pltpu.VMEM((DIFF_HEADS, l, DIFF_VD), BF16),
                        pltpu.VMEM((DIFF_HEADS, tq, DIFF_VD), F32)],
        input_output_aliases=aliases,
        compiler_params=pltpu.CompilerParams(
            dimension_semantics=("parallel", "arbitrary"), vmem_limit_bytes=VMEM_LIMIT),
        name="attn_prompt",
    )(*args)


def _attn_sample_kernel(pt_ref, q_ref, kn_ref, vn_ref, lam_ref, nw_ref, *rest, pages, lam_init):
    k_refs = rest[:pages]
    v_refs = rest[pages:2 * pages]
    o_ref = rest[2 * pages]
    m_ref, l_ref, acc_ref = rest[2 * pages + 1:]
    j = pl.program_id(1)
    nj = pl.num_programs(1)
    rows = DIFF_HEADS * 2 * 4
    rpp = PAGE_SIZE * DIFF_HEADS

    @pl.when(j == 0)
    def _():
        m_ref[...] = jnp.full((rows, 1), NEG_BIG, F32)
        l_ref[...] = jnp.zeros((rows, 1), F32)
        acc_ref[...] = jnp.zeros((rows, DIFF_VD), F32)

    q = q_ref[...]

    def update(s, pv):
        m_old = m_ref[...]
        m_new = jnp.maximum(m_old, jnp.max(s, axis=-1, keepdims=True))
        alpha = jnp.exp(m_old - m_new)
        p = jnp.exp(s - m_new)
        l_ref[...] = alpha * l_ref[...] + jnp.sum(p, axis=-1, keepdims=True)
        acc_ref[...] = alpha * acc_ref[...] + pv(p)
        m_ref[...] = m_new

    def head_rows(page_refs, h):
        return jnp.concatenate([r[pl.ds(h, PAGE_SIZE, stride=DIFF_HEADS), :].astype(BF16) for r in page_refs], axis=0)

    rph = rows // DIFF_HEADS
    s = jnp.concatenate([_nt(q[h * rph:(h + 1) * rph].astype(BF16), head_rows(k_refs, h))
                         for h in range(DIFF_HEADS)], axis=0) * DIFF_SCALE

    def pv_past(p):
        return jnp.concatenate([_nn(p[h * rph:(h + 1) * rph].astype(BF16), head_rows(v_refs, h))
                                for h in range(DIFF_HEADS)], axis=0)

    update(s, pv_past)

    @pl.when(j == nj - 1)
    def _():
        nk = kn_ref.shape[0]
        r = lax.broadcasted_iota(jnp.int32, (rows, nk), 0)
        cc = lax.broadcasted_iota(jnp.int32, (rows, nk), 1)
        ok = ((cc & (DIFF_HEADS - 1)) == (r >> 3)) & ((cc >> 2) <= (r & 3))
        sn = jnp.where(ok, _nt(q.astype(BF16), kn_ref[...].astype(BF16)) * DIFF_SCALE, NEG_BIG)
        vnb = vn_ref[...].astype(BF16)
        update(sn, lambda p: _nn(p.astype(BF16), vnb))
        lam = _diff_lambda(lam_ref, lam_init)
        o = acc_ref[...] / l_ref[...]
        nw = nw_ref[...]
        for h in range(DIFF_HEADS):
            blk = o[h * 8:(h + 1) * 8, :]
            d = blk - lam * pltpu.roll(blk, 4, 0)
            o_ref[h] = _rms_norm(d, nw) * (1.0 - lam_init)


def _attn_sample(q_rows, k_new, v_new, lam_p, nw, cache_k4, cache_v4, page_table, layer, lam_init, pages):
    b = q_rows.shape[0]
    n_pages = page_table.shape[1]
    rows = q_rows.shape[1]
    rpp = PAGE_SIZE * DIFF_HEADS

    def page_spec(i):
        return pl.BlockSpec((None, None, rpp, DIFF_VD), lambda bi, j, pt: (pt[bi, j * pages + i], layer, 0, 0))

    def per_b(shape):
        return pl.BlockSpec((None,) + shape, lambda bi, j, pt: (bi,) + (0,) * len(shape))

    kern = functools.partial(_attn_sample_kernel, pages=pages, lam_init=lam_init)
    grid_spec = pltpu.PrefetchScalarGridSpec(
        num_scalar_prefetch=1,
        grid=(b, n_pages // pages),
        in_specs=[per_b((rows, 2 * DIFF_HD)), per_b(k_new.shape[1:]), per_b(v_new.shape[1:]),
                  pl.BlockSpec((4, DIFF_HD), lambda bi, j, pt: (0, 0)),
                  pl.BlockSpec((1, DIFF_VD), lambda bi, j, pt: (0, 0))]
                 + [page_spec(i) for i in range(pages)] + [page_spec(i) for i in range(pages)],
        out_specs=per_b((DIFF_HEADS, 8, DIFF_VD)),
        scratch_shapes=[pltpu.VMEM((rows, 1), F32), pltpu.VMEM((rows, 1), F32), pltpu.VMEM((rows, DIFF_VD), F32)],
    )
    return pl.pallas_call(
        kern,
        grid_spec=grid_spec,
        out_shape=jax.ShapeDtypeStruct((b, DIFF_HEADS, 8, DIFF_VD), F32),
        compiler_params=pltpu.CompilerParams(
            dimension_semantics=("parallel", "arbitrary"), vmem_limit_bytes=VMEM_LIMIT),
        name="attn_sample",
    )(page_table, q_rows, k_new, v_new, lam_p, nw, *([cache_k4] * pages), *([cache_v4] * pages))


def _merge_mlp_kernel(x_ref, ya_ref, yb_ref, yc_ref, wg_ref, wb_ref, wo_ref, l1g_ref, l1b_ref,
                      wu_ref, wd_ref, l2g_ref, l2b_ref, o_ref, *, alpha, ff_chunk):
    x = x_ref[...]
    xb = x.astype(BF16)
    merged = None
    for i, y_ref in enumerate((ya_ref, yb_ref, yc_ref)):
        gate = jax.nn.sigmoid(_nn(xb, wg_ref[:, i * D_MODEL:(i + 1) * D_MODEL]))
        term = gate * _nn(y_ref[...].astype(BF16), wb_ref[i])
        merged = term if merged is None else merged + term
    mix = _nn(merged.astype(BF16), wo_ref[...])
    x1 = _layer_norm(alpha * x + mix, l1g_ref[...], l1b_ref[...])
    x1b = x1.astype(BF16)
    acc = None
    for cidx in range(D_FF // ff_chunk):
        up = jnp.maximum(_nn(x1b, wu_ref[:, cidx * ff_chunk:(cidx + 1) * ff_chunk]), 0.0)
        term = _nn((up * up).astype(BF16), wd_ref[cidx * ff_chunk:(cidx + 1) * ff_chunk, :])
        acc = term if acc is None else acc + term
    o_ref[...] = _layer_norm(alpha * x1 + acc, l2g_ref[...], l2b_ref[...])


def _merge_mlp(x2, ya2, yb2, yc2, wl, layer, tm, alpha):
    n = x2.shape[0]
    row = lambda w: pl.BlockSpec((tm, w), lambda i: (i, 0))
    return pl.pallas_call(
        functools.partial(_merge_mlp_kernel, alpha=alpha, ff_chunk=1024),
        grid=(n // tm,),
        in_specs=[row(D_MODEL), row(BRANCH_W), row(BRANCH_W), row(BRANCH_W),
                  _weight_cols_spec(GATE_COLS, W_GATES_AT, layer), _layer_spec((N_BRANCH, BRANCH_W, D_MODEL), layer),
                  _layer_spec((D_MODEL, D_MODEL), layer), _const_spec((1, D_MODEL)), _const_spec((1, D_MODEL)),
                  _layer_spec((D_MODEL, D_FF), layer), _layer_spec((D_FF, D_MODEL), layer),
                  _const_spec((1, D_MODEL)), _const_spec((1, D_MODEL))],
        out_specs=row(D_MODEL),
        out_shape=jax.ShapeDtypeStruct((n, D_MODEL), F32),
        compiler_params=pltpu.CompilerParams(dimension_semantics=("parallel",), vmem_limit_bytes=VMEM_LIMIT),
        name="merge_mlp",
    )(x2, ya2, yb2, yc2, wl["w_all"], wl["w_branch"], wl["w_o"], wl["ln1_g"], wl["ln1_b"],
      wl["w_up"], wl["w_down"], wl["ln2_g"], wl["ln2_b"])


def _regroup_w_in(w):
    o_ab = GDN_CONV_CH + GDN_V
    o_sb = o_ab + 2 * GDN_HEADS
    o_dq = o_sb + 3 * SC_WIDTH
    wb = w.astype(BF16)
    zeros = jnp.zeros(w.shape[:2] + (W_ATT_AT - REC_COLS + LANE - 2 * GDN_HEADS,), BF16)
    w_all = jnp.concatenate([wb[..., :o_ab], wb[..., o_sb:o_dq], wb[..., o_ab:o_sb], zeros, wb[..., o_dq:]], axis=-1)
    assert w_all.shape[-1] == W_ALL_COLS
    return w_all


def _lane_row(vals, offset):
    n = vals.shape[0]
    return jnp.pad(vals.astype(F32), (offset, LANE - offset - n)).reshape(1, LANE)


def kernel(x_prompt, x_sample, cache_k, cache_v, page_table, state_gdn, state_gdn_conv, state_sc_conv, w_in,
           gdn_conv_w, gdn_a_log, gdn_dt_bias, gdn_norm_w, sc_conv_w, diff_lambda, diff_norm_w, w_branch, w_o,
           ln1_g, ln1_b, ln2_g, ln2_b, w_up, w_down):
    depth = w_in.shape[0]
    alpha = (2 * depth) ** 0.25
    b_p, seq, _ = x_prompt.shape
    b_s, dec_seq, _ = x_sample.shape
    n_phys = cache_k.shape[0]
    assert dec_seq == 4 and seq % 256 == 0
    dec_pad = SUBLANE
    rpp = PAGE_SIZE * DIFF_HEADS
    cache_k4 = cache_k.reshape(n_phys, depth, rpp, 2 * DIFF_HD)
    cache_v4 = cache_v.reshape(n_phys, depth, rpp, DIFF_VD)
    prompt_tile = 256
    tok_tile = 512
    sample_pages = 32
    assert page_table.shape[1] % sample_pages == 0

    xp = x_prompt
    xs = jnp.pad(x_sample, ((0, 0), (0, dec_pad - dec_seq), (0, 0)))
    half = (jnp.arange(2 * DIFF_HD) < DIFF_HD)
    map_mask = jnp.stack([half, ~half]).astype(F32)
    rows_p = [[], [], [], [], []]
    rows_s = [[], [], [], [], []]
    kv_prompt = None
    w_all = _regroup_w_in(w_in)
    w_branch_b, w_o_b, w_up_b, w_down_b = (w.astype(BF16) for w in (w_branch, w_o, w_up, w_down))
    for l in range(depth):
        lam_init = 0.8 - 0.6 * math.exp(-0.3 * l)
        wl = {
            "w_all": w_all, "w_branch": w_branch_b, "w_o": w_o_b,
            "ln1_g": ln1_g[l].reshape(1, D_MODEL), "ln1_b": ln1_b[l].reshape(1, D_MODEL),
            "ln2_g": ln2_g[l].reshape(1, D_MODEL), "ln2_b": ln2_b[l].reshape(1, D_MODEL),
            "w_up": w_up_b, "w_down": w_down_b,
        }
        rec_w = (gdn_conv_w[l], _lane_row(gdn_a_log[l], GDN_HEADS), _lane_row(gdn_dt_bias[l], GDN_HEADS),
                 gdn_norm_w[l].reshape(1, GDN_DV), sc_conv_w[l])
        lam_p = diff_lambda[l].astype(F32)
        nw = diff_norm_w[l].reshape(1, DIFF_VD)

        ya, yb, s_p, gbuf_p, sbuf_p = _recurrent(
            xp, w_all, jnp.zeros((b_p, GDN_HEADS, GDN_DK, GDN_DV), F32),
            jnp.zeros((b_p, GDN_CONV - 1, GDN_CONV_CH), F32), jnp.zeros((b_p, SC_CONV - 1, SC_WIDTH), F32),
            *rec_w, layer=l, nb=2, tile=prompt_tile, chunk=GDN_CHUNK, valid=prompt_tile)
        yc, *kv_prompt = _attn_prompt(xp, w_all, lam_p, nw, lam_init, prompt_tile, l, depth, kv_prompt)
        n_p = b_p * seq
        xp = _merge_mlp(xp.reshape(n_p, D_MODEL), ya.reshape(n_p, GDN_V), yb.reshape(n_p, SC_WIDTH),
                        yc.reshape(n_p, DIFF_V), wl, l, tok_tile, alpha).reshape(b_p, seq, D_MODEL)
        rows_p[2].append(s_p)
        rows_p[3].append(gbuf_p)
        rows_p[4].append(sbuf_p)

        n_s = b_s * dec_pad
        h_rec, h_att = _in_proj(xs.reshape(n_s, D_MODEL), w_all, l)
        h_rec = h_rec.reshape(b_s, dec_pad, REC_COLS)
        h_att = h_att.reshape(b_s, dec_pad, ATT_COLS)
        ya, yb, s_s, gbuf_s, sbuf_s = _recurrent(
            h_rec, None, state_gdn[:, l], state_gdn_conv[:, l], state_sc_conv[:, l],
            *rec_w, layer=l, nb=8, tile=dec_pad, chunk=dec_pad, valid=dec_seq)
        q = h_att[:, :dec_seq, 0:DIFF_QK].reshape(b_s, dec_seq, DIFF_HEADS, 2 * DIFF_HD)
        q = jnp.transpose(q, (0, 2, 1, 3))[:, :, None] * map_mask[None, None, :, None, :]
        q_rows = q.reshape(b_s, DIFF_HEADS * 2 * dec_seq, 2 * DIFF_HD)
        k_new = h_att[:, :dec_seq, DIFF_QK:2 * DIFF_QK].reshape(b_s, dec_seq * DIFF_HEADS, 2 * DIFF_HD)
        v_new = h_att[:, :dec_seq, 2 * DIFF_QK:ATT_COLS].reshape(b_s, dec_seq * DIFF_HEADS, DIFF_VD)
        o = _attn_sample(q_rows, jnp.pad(k_new, ((0, 0), (0, LANE - dec_seq * DIFF_HEADS), (0, 0))),
                         jnp.pad(v_new, ((0, 0), (0, LANE - dec_seq * DIFF_HEADS), (0, 0))),
                         lam_p, nw, cache_k4, cache_v4, page_table, l, lam_init, pages=sample_pages)
        yc = jnp.transpose(o[:, :, :dec_seq, :], (0, 2, 1, 3)).reshape(b_s, dec_seq, DIFF_V)
        yc = jnp.pad(yc, ((0, 0), (0, dec_pad - dec_seq), (0, 0)))
        xs = _merge_mlp(xs.reshape(n_s, D_MODEL), ya.reshape(n_s, GDN_V), yb.reshape(n_s, SC_WIDTH),
                        yc.reshape(n_s, DIFF_V), wl, l, n_s, alpha).reshape(b_s, dec_pad, D_MODEL)
        rows_s[0].append(k_new.reshape(b_s, dec_seq, DIFF_HEADS, 2 * DIFF_HD))
        rows_s[1].append(v_new.reshape(b_s, dec_seq, DIFF_HEADS, DIFF_VD))
        rows_s[2].append(s_s)
        rows_s[3].append(gbuf_s)
        rows_s[4].append(sbuf_s)

    outs_p = [a.reshape(b_p, depth, seq, DIFF_HEADS, DIFF_VD) for a in kv_prompt]
    outs_p += [jnp.stack(r, axis=1) for r in rows_p[2:]]
    outs_s = [jnp.stack(r, axis=1) for r in rows_s]
    return (xp, xs[:, :dec_seq], *outs_p, *outs_s)
```

```python
import functools
import math

import jax
import jax.numpy as jnp
from jax import lax
from jax.experimental import pallas as pl
from jax.experimental.pallas import tpu as pltpu

F32 = jnp.float32
BF16 = jnp.bfloat16

D_MODEL = 1024
GDN_HEADS = 4
GDN_DK = 128
GDN_DV = 128
GDN_CONV = 4
GDN_QK = GDN_HEADS * GDN_DK
GDN_V = GDN_HEADS * GDN_DV
GDN_CONV_CH = 2 * GDN_QK + GDN_V
GDN_CHUNK = 64
SC_WIDTH = 512
SC_CONV = 3
DIFF_HEADS = 4
DIFF_HD = 64
DIFF_VD = 128
DIFF_QK = DIFF_HEADS * 2 * DIFF_HD
DIFF_V = DIFF_HEADS * DIFF_VD
DIFF_SCALE = DIFF_HD ** -0.5
PAGE_SIZE = 128
N_BRANCH = 3
BRANCH_W = 512
D_FF = 4 * D_MODEL
LN_EPS = 1e-5
RMS_EPS = 1e-6
NEG_BIG = -1e30

LANE = 128
SUBLANE = 8

COL_QKV = 0
COL_Z = COL_QKV + GDN_CONV_CH
COL_SB = COL_Z + GDN_V
COL_SC = COL_SB + SC_WIDTH
COL_SH = COL_SC + SC_WIDTH
COL_AB = COL_SH + SC_WIDTH
REC_COLS = COL_AB + LANE
ATT_COLS = DIFF_QK + DIFF_QK + DIFF_V
GATE_COLS = N_BRANCH * D_MODEL
W_REC_AT = 0
W_ATT_AT = 3 * ATT_COLS
W_GATES_AT = 2 * GATE_COLS
W_ALL_COLS = W_GATES_AT + GATE_COLS

VMEM_LIMIT = 56 * 1024 * 1024


def _nt(a, b):
    return lax.dot_general(a, b, (((1,), (1,)), ((), ())), preferred_element_type=F32)


def _tn(a, b):
    return lax.dot_general(a, b, (((0,), (0,)), ((), ())), preferred_element_type=F32)


def _nn(a, b):
    return jnp.dot(a, b, preferred_element_type=F32)


def _split3(x):
    x1 = x.astype(BF16)
    r1 = x - x1.astype(F32)
    x2 = r1.astype(BF16)
    r2 = r1 - x2.astype(F32)
    return x1, x2, r2.astype(BF16)


def _silu(x):
    h = 0.5 * x
    return h + h * jnp.tanh(h)


def _layer_norm(x, g, b):
    mu = jnp.mean(x, axis=-1, keepdims=True)
    xc = x - mu
    var = jnp.mean(xc * xc, axis=-1, keepdims=True)
    return xc * lax.rsqrt(var + LN_EPS) * g + b


def _rms_norm(x, w):
    return x * lax.rsqrt(jnp.mean(x * x, axis=-1, keepdims=True) + RMS_EPS) * w


def _const_spec(shape):
    return pl.BlockSpec(shape, lambda *_: (0,) * len(shape), pipeline_mode=pl.Buffered(1))


def _layer_spec(shape, layer):
    return pl.BlockSpec((None,) + shape, lambda *_: (layer,) + (0,) * len(shape), pipeline_mode=pl.Buffered(1))


def _weight_cols_spec(width, start, layer):
    assert start % width == 0
    return pl.BlockSpec((None, D_MODEL, width), lambda *_: (layer, 0, start // width), pipeline_mode=pl.Buffered(1))


def _in_proj_kernel(x_ref, wr_ref, wa_ref, hr_ref, ha_ref):
    xb = x_ref[...].astype(BF16)
    hr_ref[...] = _nn(xb, wr_ref[...])
    ha_ref[...] = _nn(xb, wa_ref[...])


def _in_proj(x2, w_all, layer):
    n = x2.shape[0]
    full = lambda shape: pl.BlockSpec(shape, lambda i: (0, 0))
    return pl.pallas_call(
        _in_proj_kernel,
        grid=(1,),
        in_specs=[full((n, D_MODEL)), _weight_cols_spec(REC_COLS, W_REC_AT, layer),
                  _weight_cols_spec(ATT_COLS, W_ATT_AT, layer)],
        out_specs=[full((n, REC_COLS)), full((n, ATT_COLS))],
        out_shape=[jax.ShapeDtypeStruct((n, REC_COLS), F32), jax.ShapeDtypeStruct((n, ATT_COLS), F32)],
        compiler_params=pltpu.CompilerParams(dimension_semantics=("arbitrary",), vmem_limit_bytes=VMEM_LIMIT),
        name="in_proj",
    )(x2, w_all, w_all)


def _recurrent_kernel(*refs, nb, tile, chunk, valid, project):
    if project:
        x_ref, w_ref = refs[:2]
        refs = refs[2:]
    else:
        qkv_ref, z_ref, sb_ref, sc_ref, sh_ref, ab_ref = refs[:6]
        refs = refs[6:]
    (s0_ref, gbuf0_ref, sbuf0_ref, convw_ref, alog_ref, dtb_ref, normw_ref, scw_ref,
     ya_ref, yb_ref, sout_ref, gbuf_out_ref, sbuf_out_ref,
     ext_ref, scext_ref, act_ref, state_ref) = refs[:17]
    t = pl.program_id(1)
    nt = pl.num_programs(1)
    pad = SUBLANE
    seqs = range(nb)

    @pl.when(t == 0)
    def _():
        state_ref[...] = s0_ref[...]
        for bb in seqs:
            ext_ref[bb, 0:pad, :] = jnp.zeros((pad, GDN_CONV_CH), F32)
            ext_ref[bb, pad - (GDN_CONV - 1):pad, :] = gbuf0_ref[bb]
            scext_ref[bb, 0:pad, :] = jnp.zeros((pad, SC_WIDTH), F32)
            scext_ref[bb, pad - (SC_CONV - 1):pad, :] = sbuf0_ref[bb]

    if project:
        z_ref, sb_ref, ab_ref = refs[17:20]
        xb = x_ref[...].reshape(nb * tile, D_MODEL).astype(BF16)

        def proj(lo, hi):
            return _nn(xb, w_ref[:, lo:hi]).reshape(nb, tile, hi - lo)

        ext_ref[:, pad:pad + tile, :] = proj(COL_QKV, COL_Z)
        z_ref[...] = proj(COL_Z, COL_SB)
        sb_ref[...] = proj(COL_SB, COL_SC)
        scext_ref[:, pad:pad + tile, :] = proj(COL_SC, COL_SH) * proj(COL_SH, COL_AB)
        ab_ref[...] = proj(COL_AB, REC_COLS)
    else:
        ext_ref[:, pad:pad + tile, :] = qkv_ref[...]
        scext_ref[:, pad:pad + tile, :] = sc_ref[...] * sh_ref[...]

    rb = min(tile, 64)
    convw = convw_ref[...]
    scw = scw_ref[...]
    for bb in seqs:
        for r in range(tile // rb):
            base = pad + r * rb
            a = ext_ref[bb, base - pad:base + rb, :]
            acc = a[pad:] * convw[GDN_CONV - 1:GDN_CONV, :]
            for j in range(GDN_CONV - 1):
                acc = acc + pltpu.roll(a, GDN_CONV - 1 - j, 0)[pad:] * convw[j:j + 1, :]
            act_ref[bb, r * rb:(r + 1) * rb, :] = _silu(acc)
            a = scext_ref[bb, base - pad:base + rb, :]
            u = a[pad:] * scw[SC_CONV - 1:SC_CONV, :]
            for j in range(SC_CONV - 1):
                u = u + pltpu.roll(a, SC_CONV - 1 - j, 0)[pad:] * scw[j:j + 1, :]
            yb_ref[bb, r * rb:(r + 1) * rb, :] = sb_ref[bb, r * rb:(r + 1) * rb, :] * u

    @pl.when(t == nt - 1)
    def _():
        gbuf_out_ref[...] = ext_ref[:, pad + valid - (GDN_CONV - 1):pad + valid, :]
        sbuf_out_ref[...] = scext_ref[:, pad + valid - (SC_CONV - 1):pad + valid, :]

    ext_ref[:, 0:pad, :] = ext_ref[:, tile:tile + pad, :]
    scext_ref[:, 0:pad, :] = scext_ref[:, tile:tile + pad, :]

    c = chunk
    nc = tile // c
    ri = lax.broadcasted_iota(jnp.int32, (c, c), 0)
    ci = lax.broadcasted_iota(jnp.int32, (c, c), 1)
    incl = ri >= ci
    strict = ri > ci
    tri = jnp.where(incl, 1.0, 0.0).astype(BF16)
    neg_a = -jnp.exp(alog_ref[...])
    dtb = dtb_ref[...]
    normw = normw_ref[...]
    nlev = max(1, int(math.ceil(math.log2(c))))
    rowmask = None
    if valid < tile:
        rowmask = jnp.where(lax.broadcasted_iota(jnp.int32, (c, 1), 0) < valid, 1.0, 0.0)
    heads = range(GDN_HEADS)

    blocks = [(bb, ic) for bb in seqs for ic in range(nc)]
    beta_blks, cums, cum_ts = {}, {}, {}
    for bb, ic in blocks:
        abc = ab_ref[bb, ic * c:(ic + 1) * c, :]
        beta_blk = jax.nn.sigmoid(abc)
        xg = abc + dtb
        g_blk = neg_a * (jnp.maximum(xg, 0.0) + jnp.log1p(jnp.exp(-jnp.abs(xg))))
        if rowmask is not None:
            beta_blk = beta_blk * rowmask
            g_blk = g_blk * rowmask
        g1, g2, g3 = _split3(g_blk)
        beta_blks[bb, ic] = beta_blk
        cums[bb, ic] = _nn(tri, g1) + _nn(tri, g2) + _nn(tri, g3)
        cum_ts[bb, ic] = cums[bb, ic].T

    chains = [(bb, ic, h) for bb, ic in blocks for h in heads]
    qs, ks, vs, betas, gcs, decs, kks, qks = {}, {}, {}, {}, {}, {}, {}, {}
    for key in chains:
        bb, ic, h = key
        rows = slice(ic * c, (ic + 1) * c)
        q = act_ref[bb, rows, h * GDN_DK:(h + 1) * GDN_DK]
        k = act_ref[bb, rows, GDN_QK + h * GDN_DK:GDN_QK + (h + 1) * GDN_DK]
        v = act_ref[bb, rows, 2 * GDN_QK + h * GDN_DV:2 * GDN_QK + (h + 1) * GDN_DV]
        q = q * lax.rsqrt(jnp.sum(q * q, axis=-1, keepdims=True) + RMS_EPS) * (GDN_DK ** -0.5)
        k = k * lax.rsqrt(jnp.sum(k * k, axis=-1, keepdims=True) + RMS_EPS)
        if rowmask is not None:
            k = k * rowmask
            v = v * rowmask
        gc = cums[bb, ic][:, GDN_HEADS + h:GDN_HEADS + h + 1]
        gr = cum_ts[bb, ic][GDN_HEADS + h:GDN_HEADS + h + 1, :]
        k1 = k.astype(BF16)
        qs[key], ks[key], vs[key], gcs[key] = q, k, v, gc
        betas[key] = beta_blks[bb, ic][:, h:h + 1]
        decs[key] = jnp.where(incl, jnp.exp(jnp.where(incl, gc - gr, 0.0)), 0.0)
        kks[key] = _nt(k1, k1)
        qks[key] = _nt(q.astype(BF16), k1)
    xs = {key: -jnp.where(strict, betas[key] * kks[key] * decs[key], 0.0) for key in chains}
    ams = dict(xs)
    for _ in range(nlev - 1):
        xbs = {key: xs[key].astype(BF16) for key in chains}
        xs = {key: _nn(xbs[key], xbs[key]) for key in chains}
        ams = {key: ams[key] + xs[key] + _nn(ams[key].astype(BF16), xs[key].astype(BF16)) for key in chains}
    egs = {key: jnp.exp(gcs[key]) for key in chains}
    u_bars, wq, qkb, g_lasts, k_decs = {}, {}, {}, {}, {}
    for key in chains:
        rhs = jnp.concatenate([betas[key] * vs[key], (betas[key] * egs[key]) * ks[key]], axis=-1)
        sol = rhs + _nn(ams[key].astype(BF16), rhs.astype(BF16))
        u_bars[key] = sol[:, :GDN_DV]
        wq[key] = jnp.concatenate([sol[:, GDN_DV:], egs[key] * qs[key]], axis=0).astype(BF16)
        qkb[key] = (qks[key] * decs[key]).astype(BF16)
        g_lasts[key] = gcs[key][c - 1:c, :]
        k_decs[key] = (jnp.exp(g_lasts[key] - gcs[key]) * ks[key]).astype(BF16)

    lanes = [(bb, h) for bb in seqs for h in heads]
    states = {(bb, h): state_ref[bb, h] for bb, h in lanes}
    for ic in range(nc):
        rows = slice(ic * c, (ic + 1) * c)
        sbs = {ln: states[ln].astype(BF16) for ln in lanes}
        ws = {(bb, h): _nn(wq[bb, ic, h], sbs[bb, h]) for bb, h in lanes}
        ubs = {(bb, h): (u_bars[bb, ic, h] - ws[bb, h][:c]).astype(BF16) for bb, h in lanes}
        o2 = {(bb, h): _nn(qkb[bb, ic, h], ubs[bb, h]) for bb, h in lanes}
        ds = {(bb, h): _tn(k_decs[bb, ic, h], ubs[bb, h]) for bb, h in lanes}
        for bb, h in lanes:
            states[bb, h] = jnp.exp(g_lasts[bb, ic, h]) * states[bb, h] + ds[bb, h]
            o = ws[bb, h][c:] + o2[bb, h]
            zz = z_ref[bb, rows, h * GDN_DV:(h + 1) * GDN_DV]
            ya_ref[bb, rows, h * GDN_DV:(h + 1) * GDN_DV] = _rms_norm(o, normw) * _silu(zz)
    for bb, h in lanes:
        state_ref[bb, h] = states[bb, h]

    @pl.when(t == nt - 1)
    def _():
        sout_ref[...] = state_ref[...]


def _recurrent(src, w_all, s0, gbuf0, sbuf0, convw, alog_row, dtb_row, normw, scw, *, layer, nb, tile, chunk, valid):
    b, l, _ = src.shape
    assert b % nb == 0 and l % tile == 0
    nt = l // tile
    project = w_all is not None

    def col(width, start):
        return pl.BlockSpec((nb, tile, width), lambda i, t: (i, t, start // width))

    def per_b(shape):
        return pl.BlockSpec((nb,) + shape, lambda i, t: (i,) + (0,) * len(shape))

    if project:
        src_specs = [pl.BlockSpec((nb, tile, D_MODEL), lambda i, t: (i, t, 0)), _weight_cols_spec(REC_COLS, W_REC_AT, layer)]
        src_args = [src, w_all]
        extra_scratch = [pltpu.VMEM((nb, tile, GDN_V), F32), pltpu.VMEM((nb, tile, SC_WIDTH), F32),
                         pltpu.VMEM((nb, tile, LANE), F32)]
    else:
        src_specs = [col(GDN_CONV_CH, COL_QKV), col(GDN_V, COL_Z), col(SC_WIDTH, COL_SB), col(SC_WIDTH, COL_SC),
                     col(SC_WIDTH, COL_SH), col(LANE, COL_AB)]
        src_args = [src] * 6
        extra_scratch = []
    pad = SUBLANE
    kern = functools.partial(_recurrent_kernel, nb=nb, tile=tile, chunk=chunk, valid=valid, project=project)
    return pl.pallas_call(
        kern,
        grid=(b // nb, nt),
        in_specs=src_specs + [
            per_b((GDN_HEADS, GDN_DK, GDN_DV)), per_b((GDN_CONV - 1, GDN_CONV_CH)), per_b((SC_CONV - 1, SC_WIDTH)),
            _const_spec((GDN_CONV, GDN_CONV_CH)), _const_spec((1, LANE)), _const_spec((1, LANE)),
            _const_spec((1, GDN_DV)), _const_spec((SC_CONV, SC_WIDTH))],
        out_specs=[pl.BlockSpec((nb, tile, GDN_V), lambda i, t: (i, t, 0)),
                   pl.BlockSpec((nb, tile, SC_WIDTH), lambda i, t: (i, t, 0)),
                   per_b((GDN_HEADS, GDN_DK, GDN_DV)), per_b((GDN_CONV - 1, GDN_CONV_CH)),
                   per_b((SC_CONV - 1, SC_WIDTH))],
        out_shape=[jax.ShapeDtypeStruct((b, l, GDN_V), F32), jax.ShapeDtypeStruct((b, l, SC_WIDTH), F32),
                   jax.ShapeDtypeStruct((b, GDN_HEADS, GDN_DK, GDN_DV), F32),
                   jax.ShapeDtypeStruct((b, GDN_CONV - 1, GDN_CONV_CH), F32),
                   jax.ShapeDtypeStruct((b, SC_CONV - 1, SC_WIDTH), F32)],
        scratch_shapes=[pltpu.VMEM((nb, tile + pad, GDN_CONV_CH), F32), pltpu.VMEM((nb, tile + pad, SC_WIDTH), F32),
                        pltpu.VMEM((nb, tile, GDN_CONV_CH), F32), pltpu.VMEM((nb, GDN_HEADS, GDN_DK, GDN_DV), F32)]
                       + extra_scratch,
        compiler_params=pltpu.CompilerParams(
            dimension_semantics=("parallel", "arbitrary"), vmem_limit_bytes=VMEM_LIMIT),
        name="recurrent",
    )(*src_args, s0, gbuf0, sbuf0, convw, alog_row, dtb_row, normw, scw)


def _diff_lambda(lam_ref, lam_init):
    lq = lam_ref[...]
    a = jnp.sum(lq[0:1, :] * lq[1:2, :], axis=-1, keepdims=True)
    b = jnp.sum(lq[2:3, :] * lq[3:4, :], axis=-1, keepdims=True)
    return jnp.exp(a) - jnp.exp(b) + lam_init


def _softmax_parts(s_off, s_diag):
    m = jnp.max(s_diag, axis=-1, keepdims=True)
    if s_off is not None:
        m = jnp.maximum(m, jnp.max(s_off, axis=-1, keepdims=True))
    e_diag = jnp.exp2(s_diag - m)
    total = jnp.sum(e_diag, axis=-1, keepdims=True)
    e_off = None
    if s_off is not None:
        e_off = jnp.exp2(s_off - m)
        total = total + jnp.sum(e_off, axis=-1, keepdims=True)
    return e_off, e_diag, total


def _attn_prompt_kernel(*refs, tq, nt, lam_init, aliased):
    x_ref, w_ref, lam_ref, nw_ref = refs[:4]
    refs = refs[4 + (2 if aliased else 0):]
    yc_ref, krow_ref, vrow_ref, q_s, k_s, v_s, o_s = refs
    t = pl.program_id(1)
    xb = x_ref[...].astype(BF16)
    q = _nn(xb, w_ref[:, 0:DIFF_QK])
    k = _nn(xb, w_ref[:, DIFF_QK:2 * DIFF_QK])
    v = _nn(xb, w_ref[:, 2 * DIFF_QK:ATT_COLS])
    r0 = pl.multiple_of(t * tq, tq)
    lane = lax.broadcasted_iota(jnp.int32, (1, 2 * DIFF_HD), 1)
    for h in range(DIFF_HEADS):
        qh = q[:, h * LANE:(h + 1) * LANE]
        kh = k[:, h * LANE:(h + 1) * LANE]
        vh = v[:, h * LANE:(h + 1) * LANE]
        krow_ref[pl.ds(h, tq, stride=DIFF_HEADS), :] = kh
        vrow_ref[pl.ds(h, tq, stride=DIFF_HEADS), :] = vh
        q_s[h, 0:tq, :] = jnp.where(lane < DIFF_HD, qh, 0.0).astype(BF16)
        q_s[h, tq:2 * tq, :] = jnp.where(lane >= DIFF_HD, qh, 0.0).astype(BF16)
        k_s[h, pl.ds(r0, tq), :] = kh.astype(BF16)
        v_s[h, pl.ds(r0, tq), :] = vh.astype(BF16)
    lam = _diff_lambda(lam_ref, lam_init)
    nw = nw_ref[...]
    row = lax.broadcasted_iota(jnp.int32, (2 * tq, tq), 0)
    causal = lax.broadcasted_iota(jnp.int32, (2 * tq, tq), 1) <= jnp.where(row < tq, row, row - tq)
    sc = DIFF_SCALE * math.log2(math.e)
    group = 2

    for i in range(nt):
        off = i * tq

        @pl.when(t == i)
        def _(off=off):
            def group_body(g, carry):
                hs = [g * group + u for u in range(group)]
                qs = [q_s[h] for h in hs]
                s_diag = [jnp.where(causal, _nt(qs[u], k_s[h, off:off + tq, :]) * sc, NEG_BIG)
                          for u, h in enumerate(hs)]
                s_off = [_nt(qs[u], k_s[h, 0:off, :]) * sc if off else None for u, h in enumerate(hs)]
                parts = [_softmax_parts(so, sd) for so, sd in zip(s_off, s_diag)]
                outs = [_nn(parts[u][1].astype(BF16), v_s[h, off:off + tq, :]) for u, h in enumerate(hs)]
                if off:
                    outs = [outs[u] + _nn(parts[u][0].astype(BF16), v_s[h, 0:off, :]) for u, h in enumerate(hs)]
                for u, h in enumerate(hs):
                    on = outs[u] * (1.0 / parts[u][2])
                    o = on[0:tq] - lam * on[tq:2 * tq]
                    o_s[h] = _rms_norm(o, nw) * (1.0 - lam_init)
                return carry

            lax.fori_loop(0, DIFF_HEADS // group, group_body, 0)

    for h in range(DIFF_HEADS):
        yc_ref[:, h * DIFF_VD:(h + 1) * DIFF_VD] = o_s[h]


def _attn_prompt(x3, w_all, lam_p, nw, lam_init, tq, layer, depth, kv_prev):
    b, l, _ = x3.shape
    nt = l // tq
    aliased = kv_prev is not None
    kern = functools.partial(_attn_prompt_kernel, tq=tq, nt=nt, lam_init=lam_init, aliased=aliased)
    row = lambda w: pl.BlockSpec((None, tq, w), lambda i, t: (i, t, 0))
    kv_row = pl.BlockSpec((None, None, tq * DIFF_HEADS, DIFF_VD), lambda i, t: (i, layer, t, 0))
    kv_shape = jax.ShapeDtypeStruct((b, depth, l * DIFF_HEADS, DIFF_VD), F32)
    in_specs = [row(D_MODEL), _weight_cols_spec(ATT_COLS, W_ATT_AT, layer), _const_spec((4, DIFF_HD)),
                _const_spec((1, DIFF_VD))]
    args = [x3, w_all, lam_p, nw]
    aliases = {}
    if aliased:
        in_specs += [pl.BlockSpec(memory_space=pl.ANY)] * 2
        args += list(kv_prev)
        aliases = {4: 1, 5: 2}
    return pl.pallas_call(
        kern,
        grid=(b, nt),
        in_specs=in_specs,
        out_specs=[row(DIFF_V), kv_row, kv_row],
        out_shape=[jax.ShapeDtypeStruct((b, l, DIFF_V), F32), kv_shape, kv_shape],
        scratch_shapes=[pltpu.VMEM((DIFF_HEADS, 2 * tq, 2 * DIFF_HD), BF16),
                        pltpu.VMEM((DIFF_HEADS, l, 2 * DIFF_HD), BF16), pltpu.VMEM((DIFF_HEADS, l, DIFF_VD), BF16),
                        pltpu.VMEM((DIFF_HEADS, tq, DIFF_VD), F32)],
        input_output_aliases=aliases,
        compiler_params=pltpu.CompilerParams(
            dimension_semantics=("parallel", "arbitrary"), vmem_limit_bytes=VMEM_LIMIT),
        name="attn_prompt",
    )(*args)


def _attn_sample_kernel(pt_ref, q_ref, kn_ref, vn_ref, lam_ref, nw_ref, *rest, pages, lam_init):
    k_refs = rest[:pages]
    v_refs = rest[pages:2 * pages]
    o_ref = rest[2 * pages]
    m_ref, l_ref, acc_ref = rest[2 * pages + 1:]
    j = pl.program_id(1)
    nj = pl.num_programs(1)
    rows = DIFF_HEADS * 2 * 4
    rpp = PAGE_SIZE * DIFF_HEADS

    @pl.when(j == 0)
    def _():
        m_ref[...] = jnp.full((rows, 1), NEG_BIG, F32)
        l_ref[...] = jnp.zeros((rows, 1), F32)
        acc_ref[...] = jnp.zeros((rows, DIFF_VD), F32)

    q = q_ref[...]

    def update(s, pv):
        m_old = m_ref[...]
        m_new = jnp.maximum(m_old, jnp.max(s, axis=-1, keepdims=True))
        alpha = jnp.exp(m_old - m_new)
        p = jnp.exp(s - m_new)
        l_ref[...] = alpha * l_ref[...] + jnp.sum(p, axis=-1, keepdims=True)
        acc_ref[...] = alpha * acc_ref[...] + pv(p)
        m_ref[...] = m_new

    def head_rows(page_refs, h):
        return jnp.concatenate([r[pl.ds(h, PAGE_SIZE, stride=DIFF_HEADS), :].astype(BF16) for r in page_refs], axis=0)

    rph = rows // DIFF_HEADS
    s = jnp.concatenate([_nt(q[h * rph:(h + 1) * rph].astype(BF16), head_rows(k_refs, h))
                         for h in range(DIFF_HEADS)], axis=0) * DIFF_SCALE

    def pv_past(p):
        return jnp.concatenate([_nn(p[h * rph:(h + 1) * rph].astype(BF16), head_rows(v_refs, h))
                                for h in range(DIFF_HEADS)], axis=0)

    update(s, pv_past)

    @pl.when(j == nj - 1)
    def _():
        nk = kn_ref.shape[0]
        r = lax.broadcasted_iota(jnp.int32, (rows, nk), 0)
        cc = lax.broadcasted_iota(jnp.int32, (rows, nk), 1)
        ok = ((cc & (DIFF_HEADS - 1)) == (r >> 3)) & ((cc >> 2) <= (r & 3))
        sn = jnp.where(ok, _nt(q.astype(BF16), kn_ref[...].astype(BF16)) * DIFF_SCALE, NEG_BIG)
        vnb = vn_ref[...].astype(BF16)
        update(sn, lambda p: _nn(p.astype(BF16), vnb))
        lam = _diff_lambda(lam_ref, lam_init)
        o = acc_ref[...] / l_ref[...]
        nw = nw_ref[...]
        for h in range(DIFF_HEADS):
            blk = o[h * 8:(h + 1) * 8, :]
            d = blk - lam * pltpu.roll(blk, 4, 0)
            o_ref[h] = _rms_norm(d, nw) * (1.0 - lam_init)


def _attn_sample(q_rows, k_new, v_new, lam_p, nw, cache_k4, cache_v4, page_table, layer, lam_init, pages):
    b = q_rows.shape[0]
    n_pages = page_table.shape[1]
    rows = q_rows.shape[1]
    rpp = PAGE_SIZE * DIFF_HEADS

    def page_spec(i):
        return pl.BlockSpec((None, None, rpp, DIFF_VD), lambda bi, j, pt: (pt[bi, j * pages + i], layer, 0, 0))

    def per_b(shape):
        return pl.BlockSpec((None,) + shape, lambda bi, j, pt: (bi,) + (0,) * len(shape))

    kern = functools.partial(_attn_sample_kernel, pages=pages, lam_init=lam_init)
    grid_spec = pltpu.PrefetchScalarGridSpec(
        num_scalar_prefetch=1,
        grid=(b, n_pages // pages),
        in_specs=[per_b((rows, 2 * DIFF_HD)), per_b(k_new.shape[1:]), per_b(v_new.shape[1:]),
                  pl.BlockSpec((4, DIFF_HD), lambda bi, j, pt: (0, 0)),
                  pl.BlockSpec((1, DIFF_VD), lambda bi, j, pt: (0, 0))]
                 + [page_spec(i) for i in range(pages)] + [page_spec(i) for i in range(pages)],
        out_specs=per_b((DIFF_HEADS, 8, DIFF_VD)),
        scratch_shapes=[pltpu.VMEM((rows, 1), F32), pltpu.VMEM((rows, 1), F32), pltpu.VMEM((rows, DIFF_VD), F32)],
    )
    return pl.pallas_call(
        kern,
        grid_spec=grid_spec,
        out_shape=jax.ShapeDtypeStruct((b, DIFF_HEADS, 8, DIFF_VD), F32),
        compiler_params=pltpu.CompilerParams(
            dimension_semantics=("parallel", "arbitrary"), vmem_limit_bytes=VMEM_LIMIT),
        name="attn_sample",
    )(page_table, q_rows, k_new, v_new, lam_p, nw, *([cache_k4] * pages), *([cache_v4] * pages))


def _merge_mlp_kernel(x_ref, ya_ref, yb_ref, yc_ref, wg_ref, wb_ref, wo_ref, l1g_ref, l1b_ref,
                      wu_ref, wd_ref, l2g_ref, l2b_ref, o_ref, *, alpha, ff_chunk):
    x = x_ref[...]
    xb = x.astype(BF16)
    merged = None
    for i, y_ref in enumerate((ya_ref, yb_ref, yc_ref)):
        gate = jax.nn.sigmoid(_nn(xb, wg_ref[:, i * D_MODEL:(i + 1) * D_MODEL]))
        term = gate * _nn(y_ref[...].astype(BF16), wb_ref[i])
        merged = term if merged is None else merged + term
    mix = _nn(merged.astype(BF16), wo_ref[...])
    x1 = _layer_norm(alpha * x + mix, l1g_ref[...], l1b_ref[...])
    x1b = x1.astype(BF16)
    acc = None
    for cidx in range(D_FF // ff_chunk):
        up = jnp.maximum(_nn(x1b, wu_ref[:, cidx * ff_chunk:(cidx + 1) * ff_chunk]), 0.0)
        term = _nn((up * up).astype(BF16), wd_ref[cidx * ff_chunk:(cidx + 1) * ff_chunk, :])
        acc = term if acc is None else acc + term
    o_ref[...] = _layer_norm(alpha * x1 + acc, l2g_ref[...], l2b_ref[...])


def _merge_mlp(x2, ya2, yb2, yc2, wl, layer, tm, alpha):
    n = x2.shape[0]
    row = lambda w: pl.BlockSpec((tm, w), lambda i: (i, 0))
    return pl.pallas_call(
        functools.partial(_merge_mlp_kernel, alpha=alpha, ff_chunk=1024),
        grid=(n // tm,),
        in_specs=[row(D_MODEL), row(BRANCH_W), row(BRANCH_W), row(BRANCH_W),
                  _weight_cols_spec(GATE_COLS, W_GATES_AT, layer), _layer_spec((N_BRANCH, BRANCH_W, D_MODEL), layer),
                  _layer_spec((D_MODEL, D_MODEL), layer), _const_spec((1, D_MODEL)), _const_spec((1, D_MODEL)),
                  _layer_spec((D_MODEL, D_FF), layer), _layer_spec((D_FF, D_MODEL), layer),
                  _const_spec((1, D_MODEL)), _const_spec((1, D_MODEL))],
        out_specs=row(D_MODEL),
        out_shape=jax.ShapeDtypeStruct((n, D_MODEL), F32),
        compiler_params=pltpu.CompilerParams(dimension_semantics=("parallel",), vmem_limit_bytes=VMEM_LIMIT),
        name="merge_mlp",
    )(x2, ya2, yb2, yc2, wl["w_all"], wl["w_branch"], wl["w_o"], wl["ln1_g"], wl["ln1_b"],
      wl["w_up"], wl["w_down"], wl["ln2_g"], wl["ln2_b"])


def _regroup_w_in(w):
    o_ab = GDN_CONV_CH + GDN_V
    o_sb = o_ab + 2 * GDN_HEADS
    o_dq = o_sb + 3 * SC_WIDTH
    wb = w.astype(BF16)
    zeros = jnp.zeros(w.shape[:2] + (W_ATT_AT - REC_COLS + LANE - 2 * GDN_HEADS,), BF16)
    w_all = jnp.concatenate([wb[..., :o_ab], wb[..., o_sb:o_dq], wb[..., o_ab:o_sb], zeros, wb[..., o_dq:]], axis=-1)
    assert w_all.shape[-1] == W_ALL_COLS
    return w_all


def _lane_row(vals, offset):
    n = vals.shape[0]
    return jnp.pad(vals.astype(F32), (offset, LANE - offset - n)).reshape(1, LANE)


def kernel(x_prompt, x_sample, cache_k, cache_v, page_table, state_gdn, state_gdn_conv, state_sc_conv, w_in,
           gdn_conv_w, gdn_a_log, gdn_dt_bias, gdn_norm_w, sc_conv_w, diff_lambda, diff_norm_w, w_branch, w_o,
           ln1_g, ln1_b, ln2_g, ln2_b, w_up, w_down):
    depth = w_in.shape[0]
    alpha = (2 * depth) ** 0.25
    b_p, seq, _ = x_prompt.shape
    b_s, dec_seq, _ = x_sample.shape
    n_phys = cache_k.shape[0]
    assert dec_seq == 4 and seq % 256 == 0
    dec_pad = SUBLANE
    rpp = PAGE_SIZE * DIFF_HEADS
    cache_k4 = cache_k.reshape(n_phys, depth, rpp, 2 * DIFF_HD)
    cache_v4 = cache_v.reshape(n_phys, depth, rpp, DIFF_VD)
    prompt_tile = 256
    tok_tile = 512
    sample_pages = 32
    assert page_table.shape[1] % sample_pages == 0

    xp = x_prompt
    xs = jnp.pad(x_sample, ((0, 0), (0, dec_pad - dec_seq), (0, 0)))
    half = (jnp.arange(2 * DIFF_HD) < DIFF_HD)
    map_mask = jnp.stack([half, ~half]).astype(F32)
    rows_p = [[], [], [], [], []]
    rows_s = [[], [], [], [], []]
    kv_prompt = None
    w_all = _regroup_w_in(w_in)
    w_branch_b, w_o_b, w_up_b, w_down_b = (w.astype(BF16) for w in (w_branch, w_o, w_up, w_down))
    for l in range(depth):
        lam_init = 0.8 - 0.6 * math.exp(-0.3 * l)
        wl = {
            "w_all": w_all, "w_branch": w_branch_b, "w_o": w_o_b,
            "ln1_g": ln1_g[l].reshape(1, D_MODEL), "ln1_b": ln1_b[l].reshape(1, D_MODEL),
            "ln2_g": ln2_g[l].reshape(1, D_MODEL), "ln2_b": ln2_b[l].reshape(1, D_MODEL),
            "w_up": w_up_b, "w_down": w_down_b,
        }
        rec_w = (gdn_conv_w[l], _lane_row(gdn_a_log[l], GDN_HEADS), _lane_row(gdn_dt_bias[l], GDN_HEADS),
                 gdn_norm_w[l].reshape(1, GDN_DV), sc_conv_w[l])
        lam_p = diff_lambda[l].astype(F32)
        nw = diff_norm_w[l].reshape(1, DIFF_VD)

        ya, yb, s_p, gbuf_p, sbuf_p = _recurrent(
            xp, w_all, jnp.zeros((b_p, GDN_HEADS, GDN_DK, GDN_DV), F32),
            jnp.zeros((b_p, GDN_CONV - 1, GDN_CONV_CH), F32), jnp.zeros((b_p, SC_CONV - 1, SC_WIDTH), F32),
            *rec_w, layer=l, nb=2, tile=prompt_tile, chunk=GDN_CHUNK, valid=prompt_tile)
        yc, *kv_prompt = _attn_prompt(xp, w_all, lam_p, nw, lam_init, prompt_tile, l, depth, kv_prompt)
        n_p = b_p * seq
        xp = _merge_mlp(xp.reshape(n_p, D_MODEL), ya.reshape(n_p, GDN_V), yb.reshape(n_p, SC_WIDTH),
                        yc.reshape(n_p, DIFF_V), wl, l, tok_tile, alpha).reshape(b_p, seq, D_MODEL)
        rows_p[2].append(s_p)
        rows_p[3].append(gbuf_p)
        rows_p[4].append(sbuf_p)

        n_s = b_s * dec_pad
        h_rec, h_att = _in_proj(xs.reshape(n_s, D_MODEL), w_all, l)
        h_rec = h_rec.reshape(b_s, dec_pad, REC_COLS)
        h_att = h_att.reshape(b_s, dec_pad, ATT_COLS)
        ya, yb, s_s, gbuf_s, sbuf_s = _recurrent(
            h_rec, None, state_gdn[:, l], state_gdn_conv[:, l], state_sc_conv[:, l],
            *rec_w, layer=l, nb=8, tile=dec_pad, chunk=dec_pad, valid=dec_seq)
        q = h_att[:, :dec_seq, 0:DIFF_QK].reshape(b_s, dec_seq, DIFF_HEADS, 2 * DIFF_HD)
        q = jnp.transpose(q, (0, 2, 1, 3))[:, :, None] * map_mask[None, None, :, None, :]
        q_rows = q.reshape(b_s, DIFF_HEADS * 2 * dec_seq, 2 * DIFF_HD)
        k_new = h_att[:, :dec_seq, DIFF_QK:2 * DIFF_QK].reshape(b_s, dec_seq * DIFF_HEADS, 2 * DIFF_HD)
        v_new = h_att[:, :dec_seq, 2 * DIFF_QK:ATT_COLS].reshape(b_s, dec_seq * DIFF_HEADS, DIFF_VD)
        o = _attn_sample(q_rows, jnp.pad(k_new, ((0, 0), (0, LANE - dec_seq * DIFF_HEADS), (0, 0))),
                         jnp.pad(v_new, ((0, 0), (0, LANE - dec_seq * DIFF_HEADS), (0, 0))),
                         lam_p, nw, cache_k4, cache_v4, page_table, l, lam_init, pages=sample_pages)
        yc = jnp.transpose(o[:, :, :dec_seq, :], (0, 2, 1, 3)).reshape(b_s, dec_seq, DIFF_V)
        yc = jnp.pad(yc, ((0, 0), (0, dec_pad - dec_seq), (0, 0)))
        xs = _merge_mlp(xs.reshape(n_s, D_MODEL), ya.reshape(n_s, GDN_V), yb.reshape(n_s, SC_WIDTH),
                        yc.reshape(n_s, DIFF_V), wl, l, n_s, alpha).reshape(b_s, dec_pad, D_MODEL)
        rows_s[0].append(k_new.reshape(b_s, dec_seq, DIFF_HEADS, 2 * DIFF_HD))
        rows_s[1].append(v_new.reshape(b_s, dec_seq, DIFF_HEADS, DIFF_VD))
        rows_s[2].append(s_s)
        rows_s[3].append(gbuf_s)
        rows_s[4].append(sbuf_s)

    outs_p = [a.reshape(b_p, depth, seq, DIFF_HEADS, DIFF_VD) for a in kv_prompt]
    outs_p += [jnp.stack(r, axis=1) for r in rows_p[2:]]
    outs_s = [jnp.stack(r, axis=1) for r in rows_s]
    return (xp, xs[:, :dec_seq], *outs_p, *outs_s)
```

```python
import functools
import math

import jax
import jax.numpy as jnp
from jax import lax
from jax.experimental import pallas as pl
from jax.experimental.pallas import tpu as pltpu

F32 = jnp.float32
BF16 = jnp.bfloat16

D_MODEL = 1024
GDN_HEADS = 4
GDN_DK = 128
GDN_DV = 128
GDN_CONV = 4
GDN_QK = GDN_HEADS * GDN_DK
GDN_V = GDN_HEADS * GDN_DV
GDN_CONV_CH = 2 * GDN_QK + GDN_V
GDN_CHUNK = 64
SC_WIDTH = 512
SC_CONV = 3
DIFF_HEADS = 4
DIFF_HD = 64
DIFF_VD = 128
DIFF_QK = DIFF_HEADS * 2 * DIFF_HD
DIFF_V = DIFF_HEADS * DIFF_VD
DIFF_SCALE = DIFF_HD ** -0.5
PAGE_SIZE = 128
N_BRANCH = 3
BRANCH_W = 512
D_FF = 4 * D_MODEL
LN_EPS = 1e-5
RMS_EPS = 1e-6
NEG_BIG = -1e30

LANE = 128
SUBLANE = 8

COL_QKV = 0
COL_Z = COL_QKV + GDN_CONV_CH
COL_SB = COL_Z + GDN_V
COL_SC = COL_SB + SC_WIDTH
COL_SH = COL_SC + SC_WIDTH
COL_AB = COL_SH + SC_WIDTH
REC_COLS = COL_AB + LANE
ATT_COLS = DIFF_QK + DIFF_QK + DIFF_V
GATE_COLS = N_BRANCH * D_MODEL
W_REC_AT = 0
W_ATT_AT = 3 * ATT_COLS
W_GATES_AT = 2 * GATE_COLS
W_ALL_COLS = W_GATES_AT + GATE_COLS

VMEM_LIMIT = 56 * 1024 * 1024


def _nt(a, b):
    return lax.dot_general(a, b, (((1,), (1,)), ((), ())), preferred_element_type=F32)


def _tn(a, b):
    return lax.dot_general(a, b, (((0,), (0,)), ((), ())), preferred_element_type=F32)


def _nn(a, b):
    return jnp.dot(a, b, preferred_element_type=F32)


def _split3(x):
    x1 = x.astype(BF16)
    r1 = x - x1.astype(F32)
    x2 = r1.astype(BF16)
    r2 = r1 - x2.astype(F32)
    return x1, x2, r2.astype(BF16)


def _silu(x):
    h = 0.5 * x
    return h + h * jnp.tanh(h)


def _layer_norm(x, g, b):
    mu = jnp.mean(x, axis=-1, keepdims=True)
    xc = x - mu
    var = jnp.mean(xc * xc, axis=-1, keepdims=True)
    return xc * lax.rsqrt(var + LN_EPS) * g + b


def _rms_norm(x, w):
    return x * lax.rsqrt(jnp.mean(x * x, axis=-1, keepdims=True) + RMS_EPS) * w


def _const_spec(shape):
    return pl.BlockSpec(shape, lambda *_: (0,) * len(shape), pipeline_mode=pl.Buffered(1))


def _layer_spec(shape, layer):
    return pl.BlockSpec((None,) + shape, lambda *_: (layer,) + (0,) * len(shape), pipeline_mode=pl.Buffered(1))


def _weight_cols_spec(width, start, layer):
    assert start % width == 0
    return pl.BlockSpec((None, D_MODEL, width), lambda *_: (layer, 0, start // width), pipeline_mode=pl.Buffered(1))


def _in_proj_kernel(x_ref, wr_ref, wa_ref, hr_ref, ha_ref):
    xb = x_ref[...].astype(BF16)
    hr_ref[...] = _nn(xb, wr_ref[...])
    ha_ref[...] = _nn(xb, wa_ref[...])


def _in_proj(x2, w_all, layer):
    n = x2.shape[0]
    full = lambda shape: pl.BlockSpec(shape, lambda i: (0, 0))
    return pl.pallas_call(
        _in_proj_kernel,
        grid=(1,),
        in_specs=[full((n, D_MODEL)), _weight_cols_spec(REC_COLS, W_REC_AT, layer),
                  _weight_cols_spec(ATT_COLS, W_ATT_AT, layer)],
        out_specs=[full((n, REC_COLS)), full((n, ATT_COLS))],
        out_shape=[jax.ShapeDtypeStruct((n, REC_COLS), F32), jax.ShapeDtypeStruct((n, ATT_COLS), F32)],
        compiler_params=pltpu.CompilerParams(dimension_semantics=("arbitrary",), vmem_limit_bytes=VMEM_LIMIT),
        name="in_proj",
    )(x2, w_all, w_all)


def _recurrent_kernel(*refs, nb, tile, chunk, valid, project):
    if project:
        x_ref, w_ref = refs[:2]
        refs = refs[2:]
    else:
        qkv_ref, z_ref, sb_ref, sc_ref, sh_ref, ab_ref = refs[:6]
        refs = refs[6:]
    (s0_ref, gbuf0_ref, sbuf0_ref, convw_ref, alog_ref, dtb_ref, normw_ref, scw_ref,
     ya_ref, yb_ref, sout_ref, gbuf_out_ref, sbuf_out_ref,
     ext_ref, scext_ref, act_ref, state_ref) = refs[:17]
    t = pl.program_id(1)
    nt = pl.num_programs(1)
    pad = SUBLANE
    seqs = range(nb)

    @pl.when(t == 0)
    def _():
        state_ref[...] = s0_ref[...]
        for bb in seqs:
            ext_ref[bb, 0:pad, :] = jnp.zeros((pad, GDN_CONV_CH), F32)
            ext_ref[bb, pad - (GDN_CONV - 1):pad, :] = gbuf0_ref[bb]
            scext_ref[bb, 0:pad, :] = jnp.zeros((pad, SC_WIDTH), F32)
            scext_ref[bb, pad - (SC_CONV - 1):pad, :] = sbuf0_ref[bb]

    if project:
        z_ref, sb_ref, ab_ref = refs[17:20]
        xb = x_ref[...].reshape(nb * tile, D_MODEL).astype(BF16)

        def proj(lo, hi):
            return _nn(xb, w_ref[:, lo:hi]).reshape(nb, tile, hi - lo)

        ext_ref[:, pad:pad + tile, :] = proj(COL_QKV, COL_Z)
        z_ref[...] = proj(COL_Z, COL_SB)
        sb_ref[...] = proj(COL_SB, COL_SC)
        scext_ref[:, pad:pad + tile, :] = proj(COL_SC, COL_SH) * proj(COL_SH, COL_AB)
        ab_ref[...] = proj(COL_AB, REC_COLS)
    else:
        ext_ref[:, pad:pad + tile, :] = qkv_ref[...]
        scext_ref[:, pad:pad + tile, :] = sc_ref[...] * sh_ref[...]

    rb = min(tile, 64)
    convw = convw_ref[...] * 0.5
    scw = scw_ref[...]
    for bb in seqs:
        for r in range(tile // rb):
            base = pad + r * rb
            a = ext_ref[bb, base - pad:base + rb, :]
            acc = a[pad:] * convw[GDN_CONV - 1:GDN_CONV, :]
            for j in range(GDN_CONV - 1):
                acc = acc + pltpu.roll(a, GDN_CONV - 1 - j, 0)[pad:] * convw[j:j + 1, :]
            act_ref[bb, r * rb:(r + 1) * rb, :] = acc + acc * jnp.tanh(acc)
            a = scext_ref[bb, base - pad:base + rb, :]
            u = a[pad:] * scw[SC_CONV - 1:SC_CONV, :]
            for j in range(SC_CONV - 1):
                u = u + pltpu.roll(a, SC_CONV - 1 - j, 0)[pad:] * scw[j:j + 1, :]
            yb_ref[bb, r * rb:(r + 1) * rb, :] = sb_ref[bb, r * rb:(r + 1) * rb, :] * u

    @pl.when(t == nt - 1)
    def _():
        gbuf_out_ref[...] = ext_ref[:, pad + valid - (GDN_CONV - 1):pad + valid, :]
        sbuf_out_ref[...] = scext_ref[:, pad + valid - (SC_CONV - 1):pad + valid, :]

    ext_ref[:, 0:pad, :] = ext_ref[:, tile:tile + pad, :]
    scext_ref[:, 0:pad, :] = scext_ref[:, tile:tile + pad, :]

    c = chunk
    nc = tile // c
    ri = lax.broadcasted_iota(jnp.int32, (c, c), 0)
    ci = lax.broadcasted_iota(jnp.int32, (c, c), 1)
    incl = ri >= ci
    strict = ri > ci
    tri = jnp.where(incl, 1.0, 0.0).astype(BF16)
    neg_a = -jnp.exp(alog_ref[...])
    dtb = dtb_ref[...]
    normw = normw_ref[...]
    nlev = max(1, int(math.ceil(math.log2(c))))
    rowmask = None
    if valid < tile:
        rowmask = jnp.where(lax.broadcasted_iota(jnp.int32, (c, 1), 0) < valid, 1.0, 0.0)
    heads = range(GDN_HEADS)

    blocks = [(bb, ic) for bb in seqs for ic in range(nc)]
    beta_blks, cums, cum_ts = {}, {}, {}
    for bb, ic in blocks:
        abc = ab_ref[bb, ic * c:(ic + 1) * c, :]
        beta_blk = jax.nn.sigmoid(abc)
        xg = abc + dtb
        g_blk = neg_a * (jnp.maximum(xg, 0.0) + jnp.log1p(jnp.exp(-jnp.abs(xg))))
        if rowmask is not None:
            beta_blk = beta_blk * rowmask
            g_blk = g_blk * rowmask
        g1, g2, g3 = _split3(g_blk)
        beta_blks[bb, ic] = beta_blk
        cums[bb, ic] = _nn(tri, g1) + _nn(tri, g2) + _nn(tri, g3)
        cum_ts[bb, ic] = cums[bb, ic].T

    chains = [(bb, ic, h) for bb, ic in blocks for h in heads]
    qs, ks, vs, betas, gcs, decs, kks, qks = {}, {}, {}, {}, {}, {}, {}, {}
    for key in chains:
        bb, ic, h = key
        rows = slice(ic * c, (ic + 1) * c)
        q = act_ref[bb, rows, h * GDN_DK:(h + 1) * GDN_DK]
        k = act_ref[bb, rows, GDN_QK + h * GDN_DK:GDN_QK + (h + 1) * GDN_DK]
        v = act_ref[bb, rows, 2 * GDN_QK + h * GDN_DV:2 * GDN_QK + (h + 1) * GDN_DV]
        q = q * (lax.rsqrt(jnp.sum(q * q, axis=-1, keepdims=True) + RMS_EPS) * (GDN_DK ** -0.5))
        k = k * lax.rsqrt(jnp.sum(k * k, axis=-1, keepdims=True) + RMS_EPS)
        if rowmask is not None:
            k = k * rowmask
            v = v * rowmask
        gc = cums[bb, ic][:, GDN_HEADS + h:GDN_HEADS + h + 1]
        gr = cum_ts[bb, ic][GDN_HEADS + h:GDN_HEADS + h + 1, :]
        k1 = k.astype(BF16)
        qs[key], ks[key], vs[key], gcs[key] = q, k, v, gc
        betas[key] = beta_blks[bb, ic][:, h:h + 1]
        decs[key] = jnp.exp(jnp.where(incl, gc - gr, NEG_BIG))
        kks[key] = _nt(k1, k1)
        qks[key] = _nt(q.astype(BF16), k1)
    xs = {key: jnp.where(strict, (-betas[key]) * kks[key] * decs[key], 0.0) for key in chains}
    ams = dict(xs)
    for _ in range(nlev - 1):
        xbs = {key: xs[key].astype(BF16) for key in chains}
        xs = {key: _nn(xbs[key], xbs[key]) for key in chains}
        ams = {key: ams[key] + xs[key] + _nn(ams[key].astype(BF16), xs[key].astype(BF16)) for key in chains}
    egs = {key: jnp.exp(gcs[key]) for key in chains}
    u_bars, wq, qkb, g_lasts, k_decs = {}, {}, {}, {}, {}
    for key in chains:
        rhs = jnp.concatenate([betas[key] * vs[key], (betas[key] * egs[key]) * ks[key]], axis=-1)
        sol = rhs + _nn(ams[key].astype(BF16), rhs.astype(BF16))
        u_bars[key] = sol[:, :GDN_DV]
        wq[key] = jnp.concatenate([sol[:, GDN_DV:], egs[key] * qs[key]], axis=0).astype(BF16)
        qkb[key] = (qks[key] * decs[key]).astype(BF16)
        g_lasts[key] = gcs[key][c - 1:c, :]
        k_decs[key] = (jnp.exp(g_lasts[key] - gcs[key]) * ks[key]).astype(BF16)

    lanes = [(bb, h) for bb in seqs for h in heads]
    states = {(bb, h): state_ref[bb, h] for bb, h in lanes}
    for ic in range(nc):
        rows = slice(ic * c, (ic + 1) * c)
        sbs = {ln: states[ln].astype(BF16) for ln in lanes}
        ws = {(bb, h): _nn(wq[bb, ic, h], sbs[bb, h]) for bb, h in lanes}
        ubs = {(bb, h): (u_bars[bb, ic, h] - ws[bb, h][:c]).astype(BF16) for bb, h in lanes}
        o2 = {(bb, h): _nn(qkb[bb, ic, h], ubs[bb, h]) for bb, h in lanes}
        ds = {(bb, h): _tn(k_decs[bb, ic, h], ubs[bb, h]) for bb, h in lanes}
        for bb, h in lanes:
            states[bb, h] = jnp.exp(g_lasts[bb, ic, h]) * states[bb, h] + ds[bb, h]
            o = ws[bb, h][c:] + o2[bb, h]
            zz = z_ref[bb, rows, h * GDN_DV:(h + 1) * GDN_DV]
            ya_ref[bb, rows, h * GDN_DV:(h + 1) * GDN_DV] = _rms_norm(o, normw) * _silu(zz)
    for bb, h in lanes:
        state_ref[bb, h] = states[bb, h]

    @pl.when(t == nt - 1)
    def _():
        sout_ref[...] = state_ref[...]


def _recurrent(src, w_all, s0, gbuf0, sbuf0, convw, alog_row, dtb_row, normw, scw, *, layer, nb, tile, chunk, valid):
    b, l, _ = src.shape
    assert b % nb == 0 and l % tile == 0
    nt = l // tile
    project = w_all is not None

    def col(width, start):
        return pl.BlockSpec((nb, tile, width), lambda i, t: (i, t, start // width))

    def per_b(shape):
        return pl.BlockSpec((nb,) + shape, lambda i, t: (i,) + (0,) * len(shape))

    if project:
        src_specs = [pl.BlockSpec((nb, tile, D_MODEL), lambda i, t: (i, t, 0)), _weight_cols_spec(REC_COLS, W_REC_AT, layer)]
        src_args = [src, w_all]
        extra_scratch = [pltpu.VMEM((nb, tile, GDN_V), F32), pltpu.VMEM((nb, tile, SC_WIDTH), F32),
                         pltpu.VMEM((nb, tile, LANE), F32)]
    else:
        src_specs = [col(GDN_CONV_CH, COL_QKV), col(GDN_V, COL_Z), col(SC_WIDTH, COL_SB), col(SC_WIDTH, COL_SC),
                     col(SC_WIDTH, COL_SH), col(LANE, COL_AB)]
        src_args = [src] * 6
        extra_scratch = []
    pad = SUBLANE
    kern = functools.partial(_recurrent_kernel, nb=nb, tile=tile, chunk=chunk, valid=valid, project=project)
    return pl.pallas_call(
        kern,
        grid=(b // nb, nt),
        in_specs=src_specs + [
            per_b((GDN_HEADS, GDN_DK, GDN_DV)), per_b((GDN_CONV - 1, GDN_CONV_CH)), per_b((SC_CONV - 1, SC_WIDTH)),
            _const_spec((GDN_CONV, GDN_CONV_CH)), _const_spec((1, LANE)), _const_spec((1, LANE)),
            _const_spec((1, GDN_DV)), _const_spec((SC_CONV, SC_WIDTH))],
        out_specs=[pl.BlockSpec((nb, tile, GDN_V), lambda i, t: (i, t, 0)),
                   pl.BlockSpec((nb, tile, SC_WIDTH), lambda i, t: (i, t, 0)),
                   per_b((GDN_HEADS, GDN_DK, GDN_DV)), per_b((GDN_CONV - 1, GDN_CONV_CH)),
                   per_b((SC_CONV - 1, SC_WIDTH))],
        out_shape=[jax.ShapeDtypeStruct((b, l, GDN_V), F32), jax.ShapeDtypeStruct((b, l, SC_WIDTH), F32),
                   jax.ShapeDtypeStruct((b, GDN_HEADS, GDN_DK, GDN_DV), F32),
                   jax.ShapeDtypeStruct((b, GDN_CONV - 1, GDN_CONV_CH), F32),
                   jax.ShapeDtypeStruct((b, SC_CONV - 1, SC_WIDTH), F32)],
        scratch_shapes=[pltpu.VMEM((nb, tile + pad, GDN_CONV_CH), F32), pltpu.VMEM((nb, tile + pad, SC_WIDTH), F32),
                        pltpu.VMEM((nb, tile, GDN_CONV_CH), F32), pltpu.VMEM((nb, GDN_HEADS, GDN_DK, GDN_DV), F32)]
                       + extra_scratch,
        compiler_params=pltpu.CompilerParams(
            dimension_semantics=("parallel", "arbitrary"), vmem_limit_bytes=VMEM_LIMIT),
        name="recurrent",
    )(*src_args, s0, gbuf0, sbuf0, convw, alog_row, dtb_row, normw, scw)


def _diff_lambda(lam_ref, lam_init):
    lq = lam_ref[...]
    a = jnp.sum(lq[0:1, :] * lq[1:2, :], axis=-1, keepdims=True)
    b = jnp.sum(lq[2:3, :] * lq[3:4, :], axis=-1, keepdims=True)
    return jnp.exp(a) - jnp.exp(b) + lam_init


def _softmax_parts(s_off, s_diag):
    m = jnp.max(s_diag, axis=-1, keepdims=True)
    if s_off is not None:
        m = jnp.maximum(m, jnp.max(s_off, axis=-1, keepdims=True))
    e_diag = jnp.exp2(s_diag - m)
    total = jnp.sum(e_diag, axis=-1, keepdims=True)
    e_off = None
    if s_off is not None:
        e_off = jnp.exp2(s_off - m)
        total = total + jnp.sum(e_off, axis=-1, keepdims=True)
    return e_off, e_diag, total


def _attn_prompt_kernel(*refs, tq, nt, lam_init, aliased):
    x_ref, w_ref, lam_ref, nw_ref = refs[:4]
    refs = refs[4 + (2 if aliased else 0):]
    yc_ref, krow_ref, vrow_ref, q_s, k_s, v_s, o_s = refs
    t = pl.program_id(1)
    xb = x_ref[...].astype(BF16)
    q = _nn(xb, w_ref[:, 0:DIFF_QK])
    k = _nn(xb, w_ref[:, DIFF_QK:2 * DIFF_QK])
    v = _nn(xb, w_ref[:, 2 * DIFF_QK:ATT_COLS])
    r0 = pl.multiple_of(t * tq, tq)
    lane = lax.broadcasted_iota(jnp.int32, (1, 2 * DIFF_HD), 1)
    for h in range(DIFF_HEADS):
        qh = q[:, h * LANE:(h + 1) * LANE]
        kh = k[:, h * LANE:(h + 1) * LANE]
        vh = v[:, h * LANE:(h + 1) * LANE]
        krow_ref[pl.ds(h, tq, stride=DIFF_HEADS), :] = kh
        vrow_ref[pl.ds(h, tq, stride=DIFF_HEADS), :] = vh
        q_s[h, 0:tq, :] = jnp.where(lane < DIFF_HD, qh, 0.0).astype(BF16)
        q_s[h, tq:2 * tq, :] = jnp.where(lane >= DIFF_HD, qh, 0.0).astype(BF16)
        k_s[h, pl.ds(r0, tq), :] = kh.astype(BF16)
        v_s[h, pl.ds(r0, tq), :] = vh.astype(BF16)
    lam = _diff_lambda(lam_ref, lam_init)
    nw = nw_ref[...]
    row = lax.broadcasted_iota(jnp.int32, (2 * tq, tq), 0)
    causal = lax.broadcasted_iota(jnp.int32, (2 * tq, tq), 1) <= jnp.where(row < tq, row, row - tq)
    sc = DIFF_SCALE * math.log2(math.e)
    group = 2

    for i in range(nt):
        off = i * tq

        @pl.when(t == i)
        def _(off=off):
            def group_body(g, carry):
                hs = [g * group + u for u in range(group)]
                qs = [q_s[h] for h in hs]
                s_diag = [jnp.where(causal, _nt(qs[u], k_s[h, off:off + tq, :]) * sc, NEG_BIG)
                          for u, h in enumerate(hs)]
                s_off = [_nt(qs[u], k_s[h, 0:off, :]) * sc if off else None for u, h in enumerate(hs)]
                parts = [_softmax_parts(so, sd) for so, sd in zip(s_off, s_diag)]
                outs = [_nn(parts[u][1].astype(BF16), v_s[h, off:off + tq, :]) for u, h in enumerate(hs)]
                if off:
                    outs = [outs[u] + _nn(parts[u][0].astype(BF16), v_s[h, 0:off, :]) for u, h in enumerate(hs)]
                for u, h in enumerate(hs):
                    on = outs[u] * (1.0 / parts[u][2])
                    o = on[0:tq] - lam * on[tq:2 * tq]
                    o_s[h] = _rms_norm(o, nw) * (1.0 - lam_init)
                return carry

            lax.fori_loop(0, DIFF_HEADS // group, group_body, 0)

    for h in range(DIFF_HEADS):
        yc_ref[:, h * DIFF_VD:(h + 1) * DIFF_VD] = o_s[h]


def _attn_prompt(x3, w_all, lam_p, nw, lam_init, tq, layer, depth, kv_prev):
    b, l, _ = x3.shape
    nt = l // tq
    aliased = kv_prev is not None
    kern = functools.partial(_attn_prompt_kernel, tq=tq, nt=nt, lam_init=lam_init, aliased=aliased)
    row = lambda w: pl.BlockSpec((None, tq, w), lambda i, t: (i, t, 0))
    kv_row = pl.BlockSpec((None, None, tq * DIFF_HEADS, DIFF_VD), lambda i, t: (i, layer, t, 0))
    kv_shape = jax.ShapeDtypeStruct((b, depth, l * DIFF_HEADS, DIFF_VD), F32)
    in_specs = [row(D_MODEL), _weight_cols_spec(ATT_COLS, W_ATT_AT, layer), _const_spec((4, DIFF_HD)),
                _const_spec((1, DIFF_VD))]
    args = [x3, w_all, lam_p, nw]
    aliases = {}
    if aliased:
        in_specs += [pl.BlockSpec(memory_space=pl.ANY)] * 2
        args += list(kv_prev)
        aliases = {4: 1, 5: 2}
    return pl.pallas_call(
        kern,
        grid=(b, nt),
        in_specs=in_specs,
        out_specs=[row(DIFF_V), kv_row, kv_row],
        out_shape=[jax.ShapeDtypeStruct((b, l, DIFF_V), F32), kv_shape, kv_shape],
        scratch_shapes=[pltpu.VMEM((DIFF_HEADS, 2 * tq, 2 * DIFF_HD), BF16),
                        pltpu.VMEM((DIFF_HEADS, l, 2 * DIFF_HD), BF16), pltpu.VMEM((DIFF_HEADS, l, DIFF_VD), BF16),
                        pltpu.VMEM((DIFF_HEADS, tq, DIFF_VD), F32)],
        input_output_aliases=aliases,
        compiler_params=pltpu.CompilerParams(
            dimension_semantics=("parallel", "arbitrary"), vmem_limit_bytes=VMEM_LIMIT),
        name="attn_prompt",
    )(*args)


def _attn_sample_kernel(pt_ref, q_ref, kn_ref, vn_ref, lam_ref, nw_ref, *rest, pages, lam_init):
    k_refs = rest[:pages]
    v_refs = rest[pages:2 * pages]
    o_ref = rest[2 * pages]
    m_ref, l_ref, acc_ref = rest[2 * pages + 1:]
    j = pl.program_id(1)
    nj = pl.num_programs(1)
    rows = DIFF_HEADS * 2 * 4
    rpp = PAGE_SIZE * DIFF_HEADS

    @pl.when(j == 0)
    def _():
        m_ref[...] = jnp.full((rows, 1), NEG_BIG, F32)
        l_ref[...] = jnp.zeros((rows, 1), F32)
        acc_ref[...] = jnp.zeros((rows, DIFF_VD), F32)

    q = q_ref[...]

    def update(s, pv):
        m_old = m_ref[...]
        m_new = jnp.maximum(m_old, jnp.max(s, axis=-1, keepdims=True))
        alpha = jnp.exp(m_old - m_new)
        p = jnp.exp(s - m_new)
        l_ref[...] = alpha * l_ref[...] + jnp.sum(p, axis=-1, keepdims=True)
        acc_ref[...] = alpha * acc_ref[...] + pv(p)
        m_ref[...] = m_new

    def head_rows(page_refs, h):
        return jnp.concatenate([r[pl.ds(h, PAGE_SIZE, stride=DIFF_HEADS), :].astype(BF16) for r in page_refs], axis=0)

    rph = rows // DIFF_HEADS
    s = jnp.concatenate([_nt(q[h * rph:(h + 1) * rph].astype(BF16), head_rows(k_refs, h))
                         for h in range(DIFF_HEADS)], axis=0) * DIFF_SCALE

    def pv_past(p):
        return jnp.concatenate([_nn(p[h * rph:(h + 1) * rph].astype(BF16), head_rows(v_refs, h))
                                for h in range(DIFF_HEADS)], axis=0)

    update(s, pv_past)

    @pl.when(j == nj - 1)
    def _():
        nk = kn_ref.shape[0]
        r = lax.broadcasted_iota(jnp.int32, (rows, nk), 0)
        cc = lax.broadcasted_iota(jnp.int32, (rows, nk), 1)
        ok = ((cc & (DIFF_HEADS - 1)) == (r >> 3)) & ((cc >> 2) <= (r & 3))
        sn = jnp.where(ok, _nt(q.astype(BF16), kn_ref[...].astype(BF16)) * DIFF_SCALE, NEG_BIG)
        vnb = vn_ref[...].astype(BF16)
        update(sn, lambda p: _nn(p.astype(BF16), vnb))
        lam = _diff_lambda(lam_ref, lam_init)
        o = acc_ref[...] / l_ref[...]
        nw = nw_ref[...]
        for h in range(DIFF_HEADS):
            blk = o[h * 8:(h + 1) * 8, :]
            d = blk - lam * pltpu.roll(blk, 4, 0)
            o_ref[h] = _rms_norm(d, nw) * (1.0 - lam_init)


def _attn_sample(q_rows, k_new, v_new, lam_p, nw, cache_k4, cache_v4, page_table, layer, lam_init, pages):
    b = q_rows.shape[0]
    n_pages = page_table.shape[1]
    rows = q_rows.shape[1]
    rpp = PAGE_SIZE * DIFF_HEADS

    def page_spec(i):
        return pl.BlockSpec((None, None, rpp, DIFF_VD), lambda bi, j, pt: (pt[bi, j * pages + i], layer, 0, 0))

    def per_b(shape):
        return pl.BlockSpec((None,) + shape, lambda bi, j, pt: (bi,) + (0,) * len(shape))

    kern = functools.partial(_attn_sample_kernel, pages=pages, lam_init=lam_init)
    grid_spec = pltpu.PrefetchScalarGridSpec(
        num_scalar_prefetch=1,
        grid=(b, n_pages // pages),
        in_specs=[per_b((rows, 2 * DIFF_HD)), per_b(k_new.shape[1:]), per_b(v_new.shape[1:]),
                  pl.BlockSpec((4, DIFF_HD), lambda bi, j, pt: (0, 0)),
                  pl.BlockSpec((1, DIFF_VD), lambda bi, j, pt: (0, 0))]
                 + [page_spec(i) for i in range(pages)] + [page_spec(i) for i in range(pages)],
        out_specs=per_b((DIFF_HEADS, 8, DIFF_VD)),
        scratch_shapes=[pltpu.VMEM((rows, 1), F32), pltpu.VMEM((rows, 1), F32), pltpu.VMEM((rows, DIFF_VD), F32)],
    )
    return pl.pallas_call(
        kern,
        grid_spec=grid_spec,
        out_shape=jax.ShapeDtypeStruct((b, DIFF_HEADS, 8, DIFF_VD), F32),
        compiler_params=pltpu.CompilerParams(
            dimension_semantics=("parallel", "arbitrary"), vmem_limit_bytes=VMEM_LIMIT),
        name="attn_sample",
    )(page_table, q_rows, k_new, v_new, lam_p, nw, *([cache_k4] * pages), *([cache_v4] * pages))


def _merge_mlp_kernel(x_ref, ya_ref, yb_ref, yc_ref, wg_ref, wb_ref, wo_ref, l1g_ref, l1b_ref,
                      wu_ref, wd_ref, l2g_ref, l2b_ref, o_ref, *, alpha, ff_chunk):
    x = x_ref[...]
    xb = x.astype(BF16)
    merged = None
    for i, y_ref in enumerate((ya_ref, yb_ref, yc_ref)):
        gate = jax.nn.sigmoid(_nn(xb, wg_ref[:, i * D_MODEL:(i + 1) * D_MODEL]))
        term = gate * _nn(y_ref[...].astype(BF16), wb_ref[i])
        merged = term if merged is None else merged + term
    mix = _nn(merged.astype(BF16), wo_ref[...])
    x1 = _layer_norm(alpha * x + mix, l1g_ref[...], l1b_ref[...])
    x1b = x1.astype(BF16)
    acc = None
    for cidx in range(D_FF // ff_chunk):
        up = jnp.maximum(_nn(x1b, wu_ref[:, cidx * ff_chunk:(cidx + 1) * ff_chunk]), 0.0)
        term = _nn((up * up).astype(BF16), wd_ref[cidx * ff_chunk:(cidx + 1) * ff_chunk, :])
        acc = term if acc is None else acc + term
    o_ref[...] = _layer_norm(alpha * x1 + acc, l2g_ref[...], l2b_ref[...])


def _merge_mlp(x2, ya2, yb2, yc2, wl, layer, tm, alpha):
    n = x2.shape[0]
    row = lambda w: pl.BlockSpec((tm, w), lambda i: (i, 0))
    return pl.pallas_call(
        functools.partial(_merge_mlp_kernel, alpha=alpha, ff_chunk=1024),
        grid=(n // tm,),
        in_specs=[row(D_MODEL), row(BRANCH_W), row(BRANCH_W), row(BRANCH_W),
                  _weight_cols_spec(GATE_COLS, W_GATES_AT, layer), _layer_spec((N_BRANCH, BRANCH_W, D_MODEL), layer),
                  _layer_spec((D_MODEL, D_MODEL), layer), _const_spec((1, D_MODEL)), _const_spec((1, D_MODEL)),
                  _layer_spec((D_MODEL, D_FF), layer), _layer_spec((D_FF, D_MODEL), layer),
                  _const_spec((1, D_MODEL)), _const_spec((1, D_MODEL))],
        out_specs=row(D_MODEL),
        out_shape=jax.ShapeDtypeStruct((n, D_MODEL), F32),
        compiler_params=pltpu.CompilerParams(dimension_semantics=("parallel",), vmem_limit_bytes=VMEM_LIMIT),
        name="merge_mlp",
    )(x2, ya2, yb2, yc2, wl["w_all"], wl["w_branch"], wl["w_o"], wl["ln1_g"], wl["ln1_b"],
      wl["w_up"], wl["w_down"], wl["ln2_g"], wl["ln2_b"])


def _regroup_w_in(w):
    o_ab = GDN_CONV_CH + GDN_V
    o_sb = o_ab + 2 * GDN_HEADS
    o_dq = o_sb + 3 * SC_WIDTH
    wb = w.astype(BF16)
    zeros = jnp.zeros(w.shape[:2] + (W_ATT_AT - REC_COLS + LANE - 2 * GDN_HEADS,), BF16)
    w_all = jnp.concatenate([wb[..., :o_ab], wb[..., o_sb:o_dq], wb[..., o_ab:o_sb], zeros, wb[..., o_dq:]], axis=-1)
    assert w_all.shape[-1] == W_ALL_COLS
    return w_all


def _lane_row(vals, offset):
    n = vals.shape[0]
    return jnp.pad(vals.astype(F32), (offset, LANE - offset - n)).reshape(1, LANE)


def kernel(x_prompt, x_sample, cache_k, cache_v, page_table, state_gdn, state_gdn_conv, state_sc_conv, w_in,
           gdn_conv_w, gdn_a_log, gdn_dt_bias, gdn_norm_w, sc_conv_w, diff_lambda, diff_norm_w, w_branch, w_o,
           ln1_g, ln1_b, ln2_g, ln2_b, w_up, w_down):
    depth = w_in.shape[0]
    alpha = (2 * depth) ** 0.25
    b_p, seq, _ = x_prompt.shape
    b_s, dec_seq, _ = x_sample.shape
    n_phys = cache_k.shape[0]
    assert dec_seq == 4 and seq % 256 == 0
    dec_pad = SUBLANE
    rpp = PAGE_SIZE * DIFF_HEADS
    cache_k4 = cache_k.reshape(n_phys, depth, rpp, 2 * DIFF_HD)
    cache_v4 = cache_v.reshape(n_phys, depth, rpp, DIFF_VD)
    prompt_tile = 256
    tok_tile = 512
    sample_pages = 32
    assert page_table.shape[1] % sample_pages == 0

    xp = x_prompt
    xs = jnp.pad(x_sample, ((0, 0), (0, dec_pad - dec_seq), (0, 0)))
    half = (jnp.arange(2 * DIFF_HD) < DIFF_HD)
    map_mask = jnp.stack([half, ~half]).astype(F32)
    rows_p = [[], [], [], [], []]
    rows_s = [[], [], [], [], []]
    kv_prompt = None
    w_all = _regroup_w_in(w_in)
    w_branch_b, w_o_b, w_up_b, w_down_b = (w.astype(BF16) for w in (w_branch, w_o, w_up, w_down))
    for l in range(depth):
        lam_init = 0.8 - 0.6 * math.exp(-0.3 * l)
        wl = {
            "w_all": w_all, "w_branch": w_branch_b, "w_o": w_o_b,
            "ln1_g": ln1_g[l].reshape(1, D_MODEL), "ln1_b": ln1_b[l].reshape(1, D_MODEL),
            "ln2_g": ln2_g[l].reshape(1, D_MODEL), "ln2_b": ln2_b[l].reshape(1, D_MODEL),
            "w_up": w_up_b, "w_down": w_down_b,
        }
        rec_w = (gdn_conv_w[l], _lane_row(gdn_a_log[l], GDN_HEADS), _lane_row(gdn_dt_bias[l], GDN_HEADS),
                 gdn_norm_w[l].reshape(1, GDN_DV), sc_conv_w[l])
        lam_p = diff_lambda[l].astype(F32)
        nw = diff_norm_w[l].reshape(1, DIFF_VD)

        ya, yb, s_p, gbuf_p, sbuf_p = _recurrent(
            xp, w_all, jnp.zeros((b_p, GDN_HEADS, GDN_DK, GDN_DV), F32),
            jnp.zeros((b_p, GDN_CONV - 1, GDN_CONV_CH), F32), jnp.zeros((b_p, SC_CONV - 1, SC_WIDTH), F32),
            *rec_w, layer=l, nb=2, tile=prompt_tile, chunk=GDN_CHUNK, valid=prompt_tile)
        yc, *kv_prompt = _attn_prompt(xp, w_all, lam_p, nw, lam_init, prompt_tile, l, depth, kv_prompt)
        n_p = b_p * seq
        xp = _merge_mlp(xp.reshape(n_p, D_MODEL), ya.reshape(n_p, GDN_V), yb.reshape(n_p, SC_WIDTH),
                        yc.reshape(n_p, DIFF_V), wl, l, tok_tile, alpha).reshape(b_p, seq, D_MODEL)
        rows_p[2].append(s_p)
        rows_p[3].append(gbuf_p)
        rows_p[4].append(sbuf_p)

        n_s = b_s * dec_pad
        h_rec, h_att = _in_proj(xs.reshape(n_s, D_MODEL), w_all, l)
        h_rec = h_rec.reshape(b_s, dec_pad, REC_COLS)
        h_att = h_att.reshape(b_s, dec_pad, ATT_COLS)
        ya, yb, s_s, gbuf_s, sbuf_s = _recurrent(
            h_rec, None, state_gdn[:, l], state_gdn_conv[:, l], state_sc_conv[:, l],
            *rec_w, layer=l, nb=8, tile=dec_pad, chunk=dec_pad, valid=dec_seq)
        q = h_att[:, :dec_seq, 0:DIFF_QK].reshape(b_s, dec_seq, DIFF_HEADS, 2 * DIFF_HD)
        q = jnp.transpose(q, (0, 2, 1, 3))[:, :, None] * map_mask[None, None, :, None, :]
        q_rows = q.reshape(b_s, DIFF_HEADS * 2 * dec_seq, 2 * DIFF_HD)
        k_new = h_att[:, :dec_seq, DIFF_QK:2 * DIFF_QK].reshape(b_s, dec_seq * DIFF_HEADS, 2 * DIFF_HD)
        v_new = h_att[:, :dec_seq, 2 * DIFF_QK:ATT_COLS].reshape(b_s, dec_seq * DIFF_HEADS, DIFF_VD)
        o = _attn_sample(q_rows, jnp.pad(k_new, ((0, 0), (0, LANE - dec_seq * DIFF_HEADS), (0, 0))),
                         jnp.pad(v_new, ((0, 0), (0, LANE - dec_seq * DIFF_HEADS), (0, 0))),
                         lam_p, nw, cache_k4, cache_v4, page_table, l, lam_init, pages=sample_pages)
        yc = jnp.transpose(o[:, :, :dec_seq, :], (0, 2, 1, 3)).reshape(b_s, dec_seq, DIFF_V)
        yc = jnp.pad(yc, ((0, 0), (0, dec_pad - dec_seq), (0, 0)))
        xs = _merge_mlp(xs.reshape(n_s, D_MODEL), ya.reshape(n_s, GDN_V), yb.reshape(n_s, SC_WIDTH),
                        yc.reshape(n_s, DIFF_V), wl, l, n_s, alpha).reshape(b_s, dec_pad, D_MODEL)
        rows_s[0].append(k_new.reshape(b_s, dec_seq, DIFF_HEADS, 2 * DIFF_HD))
        rows_s[1].append(v_new.reshape(b_s, dec_seq, DIFF_HEADS, DIFF_VD))
        rows_s[2].append(s_s)
        rows_s[3].append(gbuf_s)
        rows_s[4].append(sbuf_s)

    outs_p = [a.reshape(b_p, depth, seq, DIFF_HEADS, DIFF_VD) for a in kv_prompt]
    outs_p += [jnp.stack(r, axis=1) for r in rows_p[2:]]
    outs_s = [jnp.stack(r, axis=1) for r in rows_s]
    return (xp, xs[:, :dec_seq], *outs_p, *outs_s)
```

```python
import functools
import math

import jax
import jax.numpy as jnp
from jax import lax
from jax.experimental import pallas as pl
from jax.experimental.pallas import tpu as pltpu

F32 = jnp.float32
BF16 = jnp.bfloat16

D_MODEL = 1024
GDN_HEADS = 4
GDN_DK = 128
GDN_DV = 128
GDN_CONV = 4
GDN_QK = GDN_HEADS * GDN_DK
GDN_V = GDN_HEADS * GDN_DV
GDN_CONV_CH = 2 * GDN_QK + GDN_V
GDN_CHUNK = 64
SC_WIDTH = 512
SC_CONV = 3
DIFF_HEADS = 4
DIFF_HD = 64
DIFF_VD = 128
DIFF_QK = DIFF_HEADS * 2 * DIFF_HD
DIFF_V = DIFF_HEADS * DIFF_VD
DIFF_SCALE = DIFF_HD ** -0.5
PAGE_SIZE = 128
N_BRANCH = 3
BRANCH_W = 512
D_FF = 4 * D_MODEL
LN_EPS = 1e-5
RMS_EPS = 1e-6
NEG_BIG = -1e30

LANE = 128
SUBLANE = 8

COL_QKV = 0
COL_Z = COL_QKV + GDN_CONV_CH
COL_SB = COL_Z + GDN_V
COL_SC = COL_SB + SC_WIDTH
COL_SH = COL_SC + SC_WIDTH
COL_AB = COL_SH + SC_WIDTH
REC_COLS = COL_AB + LANE
ATT_COLS = DIFF_QK + DIFF_QK + DIFF_V
GATE_COLS = N_BRANCH * D_MODEL
W_REC_AT = 0
W_ATT_AT = 3 * ATT_COLS
W_GATES_AT = 2 * GATE_COLS
W_ALL_COLS = W_GATES_AT + GATE_COLS

VMEM_LIMIT = 56 * 1024 * 1024


def _nt(a, b):
    return lax.dot_general(a, b, (((1,), (1,)), ((), ())), preferred_element_type=F32)


def _tn(a, b):
    return lax.dot_general(a, b, (((0,), (0,)), ((), ())), preferred_element_type=F32)


def _nn(a, b):
    return jnp.dot(a, b, preferred_element_type=F32)


def _split3(x):
    x1 = x.astype(BF16)
    r1 = x - x1.astype(F32)
    x2 = r1.astype(BF16)
    r2 = r1 - x2.astype(F32)
    return x1, x2, r2.astype(BF16)


def _silu(x):
    h = 0.5 * x
    return h + h * jnp.tanh(h)


def _layer_norm(x, g, b):
    mu = jnp.mean(x, axis=-1, keepdims=True)
    xc = x - mu
    var = jnp.mean(xc * xc, axis=-1, keepdims=True)
    return xc * lax.rsqrt(var + LN_EPS) * g + b


def _rms_norm(x, w):
    return x * lax.rsqrt(jnp.mean(x * x, axis=-1, keepdims=True) + RMS_EPS) * w


def _const_spec(shape):
    return pl.BlockSpec(shape, lambda *_: (0,) * len(shape), pipeline_mode=pl.Buffered(1))


def _layer_spec(shape, layer):
    return pl.BlockSpec((None,) + shape, lambda *_: (layer,) + (0,) * len(shape), pipeline_mode=pl.Buffered(1))


def _weight_cols_spec(width, start, layer):
    assert start % width == 0
    return pl.BlockSpec((None, D_MODEL, width), lambda *_: (layer, 0, start // width), pipeline_mode=pl.Buffered(1))


def _in_proj_kernel(x_ref, wr_ref, wa_ref, hr_ref, ha_ref):
    xb = x_ref[...].astype(BF16)
    hr_ref[...] = _nn(xb, wr_ref[...])
    ha_ref[...] = _nn(xb, wa_ref[...])


def _in_proj(x2, w_all, layer):
    n = x2.shape[0]
    full = lambda shape: pl.BlockSpec(shape, lambda i: (0, 0))
    return pl.pallas_call(
        _in_proj_kernel,
        grid=(1,),
        in_specs=[full((n, D_MODEL)), _weight_cols_spec(REC_COLS, W_REC_AT, layer),
                  _weight_cols_spec(ATT_COLS, W_ATT_AT, layer)],
        out_specs=[full((n, REC_COLS)), full((n, ATT_COLS))],
        out_shape=[jax.ShapeDtypeStruct((n, REC_COLS), F32), jax.ShapeDtypeStruct((n, ATT_COLS), F32)],
        compiler_params=pltpu.CompilerParams(dimension_semantics=("arbitrary",), vmem_limit_bytes=VMEM_LIMIT),
        name="in_proj",
    )(x2, w_all, w_all)


def _recurrent_kernel(*refs, nb, tile, chunk, valid, project):
    if project:
        x_ref, w_ref = refs[:2]
        refs = refs[2:]
    else:
        qkv_ref, z_ref, sb_ref, sc_ref, sh_ref, ab_ref = refs[:6]
        refs = refs[6:]
    (s0_ref, gbuf0_ref, sbuf0_ref, convw_ref, alog_ref, dtb_ref, normw_ref, scw_ref,
     ya_ref, yb_ref, sout_ref, gbuf_out_ref, sbuf_out_ref,
     ext_ref, scext_ref, act_ref, state_ref) = refs[:17]
    t = pl.program_id(1)
    nt = pl.num_programs(1)
    pad = SUBLANE
    seqs = range(nb)

    @pl.when(t == 0)
    def _():
        state_ref[...] = s0_ref[...]
        for bb in seqs:
            ext_ref[bb, 0:pad, :] = jnp.zeros((pad, GDN_CONV_CH), F32)
            ext_ref[bb, pad - (GDN_CONV - 1):pad, :] = gbuf0_ref[bb]
            scext_ref[bb, 0:pad, :] = jnp.zeros((pad, SC_WIDTH), F32)
            scext_ref[bb, pad - (SC_CONV - 1):pad, :] = sbuf0_ref[bb]

    if project:
        z_ref, sb_ref, ab_ref = refs[17:20]
        xb = x_ref[...].reshape(nb * tile, D_MODEL).astype(BF16)

        def proj(lo, hi):
            return _nn(xb, w_ref[:, lo:hi]).reshape(nb, tile, hi - lo)

        ext_ref[:, pad:pad + tile, :] = proj(COL_QKV, COL_Z)
        z_ref[...] = proj(COL_Z, COL_SB)
        sb_ref[...] = proj(COL_SB, COL_SC)
        scext_ref[:, pad:pad + tile, :] = proj(COL_SC, COL_SH) * proj(COL_SH, COL_AB)
        ab_ref[...] = proj(COL_AB, REC_COLS)
    else:
        ext_ref[:, pad:pad + tile, :] = qkv_ref[...]
        scext_ref[:, pad:pad + tile, :] = sc_ref[...] * sh_ref[...]

    rb = min(tile, 64)
    convw = convw_ref[...] * 0.5
    scw = scw_ref[...]
    for bb in seqs:
        for r in range(tile // rb):
            base = pad + r * rb
            a = ext_ref[bb, base - pad:base + rb, :]
            acc = a[pad:] * convw[GDN_CONV - 1:GDN_CONV, :]
            for j in range(GDN_CONV - 1):
                acc = acc + pltpu.roll(a, GDN_CONV - 1 - j, 0)[pad:] * convw[j:j + 1, :]
            act_ref[bb, r * rb:(r + 1) * rb, :] = acc + acc * jnp.tanh(acc)
            a = scext_ref[bb, base - pad:base + rb, :]
            u = a[pad:] * scw[SC_CONV - 1:SC_CONV, :]
            for j in range(SC_CONV - 1):
                u = u + pltpu.roll(a, SC_CONV - 1 - j, 0)[pad:] * scw[j:j + 1, :]
            yb_ref[bb, r * rb:(r + 1) * rb, :] = sb_ref[bb, r * rb:(r + 1) * rb, :] * u

    @pl.when(t == nt - 1)
    def _():
        gbuf_out_ref[...] = ext_ref[:, pad + valid - (GDN_CONV - 1):pad + valid, :]
        sbuf_out_ref[...] = scext_ref[:, pad + valid - (SC_CONV - 1):pad + valid, :]

    ext_ref[:, 0:pad, :] = ext_ref[:, tile:tile + pad, :]
    scext_ref[:, 0:pad, :] = scext_ref[:, tile:tile + pad, :]

    c = chunk
    nc = tile // c
    ri = lax.broadcasted_iota(jnp.int32, (c, c), 0)
    ci = lax.broadcasted_iota(jnp.int32, (c, c), 1)
    incl = ri >= ci
    strict = ri > ci
    tri = jnp.where(incl, 1.0, 0.0).astype(BF16)
    neg_a = -jnp.exp(alog_ref[...])
    dtb = dtb_ref[...]
    normw = normw_ref[...]
    nlev = max(1, int(math.ceil(math.log2(c))))
    rowmask = None
    if valid < tile:
        rowmask = jnp.where(lax.broadcasted_iota(jnp.int32, (c, 1), 0) < valid, 1.0, 0.0)
    heads = range(GDN_HEADS)

    blocks = [(bb, ic) for bb in seqs for ic in range(nc)]
    beta_blks, cums, cum_ts = {}, {}, {}
    for bb, ic in blocks:
        abc = ab_ref[bb, ic * c:(ic + 1) * c, :]
        beta_blk = jax.nn.sigmoid(abc)
        xg = abc + dtb
        g_blk = neg_a * (jnp.maximum(xg, 0.0) + jnp.log1p(jnp.exp(-jnp.abs(xg))))
        if rowmask is not None:
            beta_blk = beta_blk * rowmask
            g_blk = g_blk * rowmask
        g1, g2, g3 = _split3(g_blk)
        beta_blks[bb, ic] = beta_blk
        cums[bb, ic] = _nn(tri, g1) + _nn(tri, g2) + _nn(tri, g3)
        cum_ts[bb, ic] = cums[bb, ic].T

    chains = [(bb, ic, h) for bb, ic in blocks for h in heads]
    qs, ks, vs, betas, gcs, decs, kks, qks = {}, {}, {}, {}, {}, {}, {}, {}
    for key in chains:
        bb, ic, h = key
        rows = slice(ic * c, (ic + 1) * c)
        q = act_ref[bb, rows, h * GDN_DK:(h + 1) * GDN_DK]
        k = act_ref[bb, rows, GDN_QK + h * GDN_DK:GDN_QK + (h + 1) * GDN_DK]
        v = act_ref[bb, rows, 2 * GDN_QK + h * GDN_DV:2 * GDN_QK + (h + 1) * GDN_DV]
        q = q * (lax.rsqrt(jnp.sum(q * q, axis=-1, keepdims=True) + RMS_EPS) * (GDN_DK ** -0.5))
        k = k * lax.rsqrt(jnp.sum(k * k, axis=-1, keepdims=True) + RMS_EPS)
        if rowmask is not None:
            k = k * rowmask
            v = v * rowmask
        gc = cums[bb, ic][:, GDN_HEADS + h:GDN_HEADS + h + 1]
        gr = cum_ts[bb, ic][GDN_HEADS + h:GDN_HEADS + h + 1, :]
        k1 = k.astype(BF16)
        qs[key], ks[key], vs[key], gcs[key] = q, k, v, gc
        betas[key] = beta_blks[bb, ic][:, h:h + 1]
        decs[key] = jnp.exp(jnp.where(incl, gc - gr, NEG_BIG))
        kks[key] = _nt(k1, k1)
        qks[key] = _nt(q.astype(BF16), k1)
    xs = {key: jnp.where(strict, (-betas[key]) * kks[key] * decs[key], 0.0) for key in chains}
    ams = dict(xs)
    for _ in range(nlev - 1):
        xbs = {key: xs[key].astype(BF16) for key in chains}
        xs = {key: _nn(xbs[key], xbs[key]) for key in chains}
        ams = {key: ams[key] + xs[key] + _nn(ams[key].astype(BF16), xs[key].astype(BF16)) for key in chains}
    egs = {key: jnp.exp(gcs[key]) for key in chains}
    u_bars, wq, qkb, g_lasts, k_decs = {}, {}, {}, {}, {}
    for key in chains:
        rhs = jnp.concatenate([betas[key] * vs[key], (betas[key] * egs[key]) * ks[key]], axis=-1)
        sol = rhs + _nn(ams[key].astype(BF16), rhs.astype(BF16))
        u_bars[key] = sol[:, :GDN_DV]
        wq[key] = jnp.concatenate([sol[:, GDN_DV:], egs[key] * qs[key]], axis=0).astype(BF16)
        qkb[key] = (qks[key] * decs[key]).astype(BF16)
        g_lasts[key] = gcs[key][c - 1:c, :]
        k_decs[key] = (jnp.exp(g_lasts[key] - gcs[key]) * ks[key]).astype(BF16)

    lanes = [(bb, h) for bb in seqs for h in heads]
    states = {(bb, h): state_ref[bb, h] for bb, h in lanes}
    for ic in range(nc):
        rows = slice(ic * c, (ic + 1) * c)
        sbs = {ln: states[ln].astype(BF16) for ln in lanes}
        ws = {(bb, h): _nn(wq[bb, ic, h], sbs[bb, h]) for bb, h in lanes}
        ubs = {(bb, h): (u_bars[bb, ic, h] - ws[bb, h][:c]).astype(BF16) for bb, h in lanes}
        o2 = {(bb, h): _nn(qkb[bb, ic, h], ubs[bb, h]) for bb, h in lanes}
        ds = {(bb, h): _tn(k_decs[bb, ic, h], ubs[bb, h]) for bb, h in lanes}
        for bb, h in lanes:
            states[bb, h] = jnp.exp(g_lasts[bb, ic, h]) * states[bb, h] + ds[bb, h]
            o = ws[bb, h][c:] + o2[bb, h]
            zz = z_ref[bb, rows, h * GDN_DV:(h + 1) * GDN_DV]
            ya_ref[bb, rows, h * GDN_DV:(h + 1) * GDN_DV] = _rms_norm(o, normw) * _silu(zz)
    for bb, h in lanes:
        state_ref[bb, h] = states[bb, h]

    @pl.when(t == nt - 1)
    def _():
        sout_ref[...] = state_ref[...]


def _recurrent(src, w_all, s0, gbuf0, sbuf0, convw, alog_row, dtb_row, normw, scw, *, layer, nb, tile, chunk, valid):
    b, l, _ = src.shape
    assert b % nb == 0 and l % tile == 0
    nt = l // tile
    project = w_all is not None

    def col(width, start):
        return pl.BlockSpec((nb, tile, width), lambda i, t: (i, t, start // width))

    def per_b(shape):
        return pl.BlockSpec((nb,) + shape, lambda i, t: (i,) + (0,) * len(shape))

    if project:
        src_specs = [pl.BlockSpec((nb, tile, D_MODEL), lambda i, t: (i, t, 0)), _weight_cols_spec(REC_COLS, W_REC_AT, layer)]
        src_args = [src, w_all]
        extra_scratch = [pltpu.VMEM((nb, tile, GDN_V), F32), pltpu.VMEM((nb, tile, SC_WIDTH), F32),
                         pltpu.VMEM((nb, tile, LANE), F32)]
    else:
        src_specs = [col(GDN_CONV_CH, COL_QKV), col(GDN_V, COL_Z), col(SC_WIDTH, COL_SB), col(SC_WIDTH, COL_SC),
                     col(SC_WIDTH, COL_SH), col(LANE, COL_AB)]
        src_args = [src] * 6
        extra_scratch = []
    pad = SUBLANE
    kern = functools.partial(_recurrent_kernel, nb=nb, tile=tile, chunk=chunk, valid=valid, project=project)
    return pl.pallas_call(
        kern,
        grid=(b // nb, nt),
        in_specs=src_specs + [
            per_b((GDN_HEADS, GDN_DK, GDN_DV)), per_b((GDN_CONV - 1, GDN_CONV_CH)), per_b((SC_CONV - 1, SC_WIDTH)),
            _const_spec((GDN_CONV, GDN_CONV_CH)), _const_spec((1, LANE)), _const_spec((1, LANE)),
            _const_spec((1, GDN_DV)), _const_spec((SC_CONV, SC_WIDTH))],
        out_specs=[pl.BlockSpec((nb, tile, GDN_V), lambda i, t: (i, t, 0)),
                   pl.BlockSpec((nb, tile, SC_WIDTH), lambda i, t: (i, t, 0)),
                   per_b((GDN_HEADS, GDN_DK, GDN_DV)), per_b((GDN_CONV - 1, GDN_CONV_CH)),
                   per_b((SC_CONV - 1, SC_WIDTH))],
        out_shape=[jax.ShapeDtypeStruct((b, l, GDN_V), F32), jax.ShapeDtypeStruct((b, l, SC_WIDTH), F32),
                   jax.ShapeDtypeStruct((b, GDN_HEADS, GDN_DK, GDN_DV), F32),
                   jax.ShapeDtypeStruct((b, GDN_CONV - 1, GDN_CONV_CH), F32),
                   jax.ShapeDtypeStruct((b, SC_CONV - 1, SC_WIDTH), F32)],
        scratch_shapes=[pltpu.VMEM((nb, tile + pad, GDN_CONV_CH), F32), pltpu.VMEM((nb, tile + pad, SC_WIDTH), F32),
                        pltpu.VMEM((nb, tile, GDN_CONV_CH), F32), pltpu.VMEM((nb, GDN_HEADS, GDN_DK, GDN_DV), F32)]
                       + extra_scratch,
        compiler_params=pltpu.CompilerParams(
            dimension_semantics=("parallel", "arbitrary"), vmem_limit_bytes=VMEM_LIMIT),
        name="recurrent",
    )(*src_args, s0, gbuf0, sbuf0, convw, alog_row, dtb_row, normw, scw)


def _diff_lambda(lam_ref, lam_init):
    lq = lam_ref[...]
    a = jnp.sum(lq[0:1, :] * lq[1:2, :], axis=-1, keepdims=True)
    b = jnp.sum(lq[2:3, :] * lq[3:4, :], axis=-1, keepdims=True)
    return jnp.exp(a) - jnp.exp(b) + lam_init


def _softmax_parts(s_off, s_diag):
    m = jnp.max(s_diag, axis=-1, keepdims=True)
    if s_off is not None:
        m = jnp.maximum(m, jnp.max(s_off, axis=-1, keepdims=True))
    e_diag = jnp.exp2(s_diag - m)
    total = jnp.sum(e_diag, axis=-1, keepdims=True)
    e_off = None
    if s_off is not None:
        e_off = jnp.exp2(s_off - m)
        total = total + jnp.sum(e_off, axis=-1, keepdims=True)
    return e_off, e_diag, total


def _attn_prompt_kernel(*refs, tq, nt, lam_init, aliased):
    x_ref, w_ref, lam_ref, nw_ref = refs[:4]
    refs = refs[4 + (2 if aliased else 0):]
    yc_ref, krow_ref, vrow_ref, q_s, k_s, v_s, o_s = refs
    t = pl.program_id(1)
    xb = x_ref[...].astype(BF16)
    q = _nn(xb, w_ref[:, 0:DIFF_QK])
    k = _nn(xb, w_ref[:, DIFF_QK:2 * DIFF_QK])
    v = _nn(xb, w_ref[:, 2 * DIFF_QK:ATT_COLS])
    r0 = pl.multiple_of(t * tq, tq)
    lane = lax.broadcasted_iota(jnp.int32, (1, 2 * DIFF_HD), 1)
    sc = DIFF_SCALE * math.log2(math.e)
    for h in range(DIFF_HEADS):
        qh = q[:, h * LANE:(h + 1) * LANE] * sc
        kh = k[:, h * LANE:(h + 1) * LANE]
        vh = v[:, h * LANE:(h + 1) * LANE]
        krow_ref[pl.ds(h, tq, stride=DIFF_HEADS), :] = kh
        vrow_ref[pl.ds(h, tq, stride=DIFF_HEADS), :] = vh
        q_s[h, 0:tq, :] = jnp.where(lane < DIFF_HD, qh, 0.0).astype(BF16)
        q_s[h, tq:2 * tq, :] = jnp.where(lane >= DIFF_HD, qh, 0.0).astype(BF16)
        k_s[h, pl.ds(r0, tq), :] = kh.astype(BF16)
        v_s[h, pl.ds(r0, tq), :] = vh.astype(BF16)
    lam = _diff_lambda(lam_ref, lam_init)
    nw = nw_ref[...]
    row = lax.broadcasted_iota(jnp.int32, (2 * tq, tq), 0)
    causal = lax.broadcasted_iota(jnp.int32, (2 * tq, tq), 1) <= jnp.where(row < tq, row, row - tq)
    group = 2

    for i in range(nt):
        off = i * tq

        @pl.when(t == i)
        def _(off=off):
            def group_body(g, carry):
                hs = [g * group + u for u in range(group)]
                qs = [q_s[h] for h in hs]
                s_diag = [jnp.where(causal, _nt(qs[u], k_s[h, off:off + tq, :]), NEG_BIG)
                          for u, h in enumerate(hs)]
                s_off = [_nt(qs[u], k_s[h, 0:off, :]) if off else None for u, h in enumerate(hs)]
                parts = [_softmax_parts(so, sd) for so, sd in zip(s_off, s_diag)]
                outs = [_nn(parts[u][1].astype(BF16), v_s[h, off:off + tq, :]) for u, h in enumerate(hs)]
                if off:
                    outs = [outs[u] + _nn(parts[u][0].astype(BF16), v_s[h, 0:off, :]) for u, h in enumerate(hs)]
                for u, h in enumerate(hs):
                    on = outs[u] * (1.0 / parts[u][2])
                    o = on[0:tq] - lam * on[tq:2 * tq]
                    o_s[h] = _rms_norm(o, nw) * (1.0 - lam_init)
                return carry

            lax.fori_loop(0, DIFF_HEADS // group, group_body, 0)

    for h in range(DIFF_HEADS):
        yc_ref[:, h * DIFF_VD:(h + 1) * DIFF_VD] = o_s[h]


def _attn_prompt(x3, w_all, lam_p, nw, lam_init, tq, layer, depth, kv_prev):
    b, l, _ = x3.shape
    nt = l // tq
    aliased = kv_prev is not None
    kern = functools.partial(_attn_prompt_kernel, tq=tq, nt=nt, lam_init=lam_init, aliased=aliased)
    row = lambda w: pl.BlockSpec((None, tq, w), lambda i, t: (i, t, 0))
    kv_row = pl.BlockSpec((None, None, tq * DIFF_HEADS, DIFF_VD), lambda i, t: (i, layer, t, 0))
    kv_shape = jax.ShapeDtypeStruct((b, depth, l * DIFF_HEADS, DIFF_VD), F32)
    in_specs = [row(D_MODEL), _weight_cols_spec(ATT_COLS, W_ATT_AT, layer), _const_spec((4, DIFF_HD)),
                _const_spec((1, DIFF_VD))]
    args = [x3, w_all, lam_p, nw]
    aliases = {}
    if aliased:
        in_specs += [pl.BlockSpec(memory_space=pl.ANY)] * 2
        args += list(kv_prev)
        aliases = {4: 1, 5: 2}
    return pl.pallas_call(
        kern,
        grid=(b, nt),
        in_specs=in_specs,
        out_specs=[row(DIFF_V), kv_row, kv_row],
        out_shape=[jax.ShapeDtypeStruct((b, l, DIFF_V), F32), kv_shape, kv_shape],
        scratch_shapes=[pltpu.VMEM((DIFF_HEADS, 2 * tq, 2 * DIFF_HD), BF16),
                        pltpu.VMEM((DIFF_HEADS, l, 2 * DIFF_HD), BF16), pltpu.VMEM((DIFF_HEADS, l, DIFF_VD), BF16),
                        pltpu.VMEM((DIFF_HEADS, tq, DIFF_VD), F32)],
        input_output_aliases=aliases,
        compiler_params=pltpu.CompilerParams(
            dimension_semantics=("parallel", "arbitrary"), vmem_limit_bytes=VMEM_LIMIT),
        name="attn_prompt",
    )(*args)


def _attn_sample_kernel(pt_ref, q_ref, kn_ref, vn_ref, lam_ref, nw_ref, *rest, pages, lam_init):
    k_refs = rest[:pages]
    v_refs = rest[pages:2 * pages]
    o_ref = rest[2 * pages]
    m_ref, l_ref, acc_ref = rest[2 * pages + 1:]
    j = pl.program_id(1)
    nj = pl.num_programs(1)
    rows = DIFF_HEADS * 2 * 4
    rpp = PAGE_SIZE * DIFF_HEADS

    @pl.when(j == 0)
    def _():
        m_ref[...] = jnp.full((rows, 1), NEG_BIG, F32)
        l_ref[...] = jnp.zeros((rows, 1), F32)
        acc_ref[...] = jnp.zeros((rows, DIFF_VD), F32)

    q = q_ref[...]

    def update(s, pv):
        m_old = m_ref[...]
        m_new = jnp.maximum(m_old, jnp.max(s, axis=-1, keepdims=True))
        alpha = jnp.exp(m_old - m_new)
        p = jnp.exp(s - m_new)
        l_ref[...] = alpha * l_ref[...] + jnp.sum(p, axis=-1, keepdims=True)
        acc_ref[...] = alpha * acc_ref[...] + pv(p)
        m_ref[...] = m_new

    def head_rows(page_refs, h):
        return jnp.concatenate([r[pl.ds(h, PAGE_SIZE, stride=DIFF_HEADS), :].astype(BF16) for r in page_refs], axis=0)

    rph = rows // DIFF_HEADS
    s = jnp.concatenate([_nt(q[h * rph:(h + 1) * rph].astype(BF16), head_rows(k_refs, h))
                         for h in range(DIFF_HEADS)], axis=0) * DIFF_SCALE

    def pv_past(p):
        return jnp.concatenate([_nn(p[h * rph:(h + 1) * rph].astype(BF16), head_rows(v_refs, h))
                                for h in range(DIFF_HEADS)], axis=0)

    update(s, pv_past)

    @pl.when(j == nj - 1)
    def _():
        nk = kn_ref.shape[0]
        r = lax.broadcasted_iota(jnp.int32, (rows, nk), 0)
        cc = lax.broadcasted_iota(jnp.int32, (rows, nk), 1)
        ok = ((cc & (DIFF_HEADS - 1)) == (r >> 3)) & ((cc >> 2) <= (r & 3))
        sn = jnp.where(ok, _nt(q.astype(BF16), kn_ref[...].astype(BF16)) * DIFF_SCALE, NEG_BIG)
        vnb = vn_ref[...].astype(BF16)
        update(sn, lambda p: _nn(p.astype(BF16), vnb))
        lam = _diff_lambda(lam_ref, lam_init)
        o = acc_ref[...] / l_ref[...]
        nw = nw_ref[...]
        for h in range(DIFF_HEADS):
            blk = o[h * 8:(h + 1) * 8, :]
            d = blk - lam * pltpu.roll(blk, 4, 0)
            o_ref[h] = _rms_norm(d, nw) * (1.0 - lam_init)


def _attn_sample(q_rows, k_new, v_new, lam_p, nw, cache_k4, cache_v4, page_table, layer, lam_init, pages):
    b = q_rows.shape[0]
    n_pages = page_table.shape[1]
    rows = q_rows.shape[1]
    rpp = PAGE_SIZE * DIFF_HEADS

    def page_spec(i):
        return pl.BlockSpec((None, None, rpp, DIFF_VD), lambda bi, j, pt: (pt[bi, j * pages + i], layer, 0, 0))

    def per_b(shape):
        return pl.BlockSpec((None,) + shape, lambda bi, j, pt: (bi,) + (0,) * len(shape))

    kern = functools.partial(_attn_sample_kernel, pages=pages, lam_init=lam_init)
    grid_spec = pltpu.PrefetchScalarGridSpec(
        num_scalar_prefetch=1,
        grid=(b, n_pages // pages),
        in_specs=[per_b((rows, 2 * DIFF_HD)), per_b(k_new.shape[1:]), per_b(v_new.shape[1:]),
                  pl.BlockSpec((4, DIFF_HD), lambda bi, j, pt: (0, 0)),
                  pl.BlockSpec((1, DIFF_VD), lambda bi, j, pt: (0, 0))]
                 + [page_spec(i) for i in range(pages)] + [page_spec(i) for i in range(pages)],
        out_specs=per_b((DIFF_HEADS, 8, DIFF_VD)),
        scratch_shapes=[pltpu.VMEM((rows, 1), F32), pltpu.VMEM((rows, 1), F32), pltpu.VMEM((rows, DIFF_VD), F32)],
    )
    return pl.pallas_call(
        kern,
        grid_spec=grid_spec,
        out_shape=jax.ShapeDtypeStruct((b, DIFF_HEADS, 8, DIFF_VD), F32),
        compiler_params=pltpu.CompilerParams(
            dimension_semantics=("parallel", "arbitrary"), vmem_limit_bytes=VMEM_LIMIT),
        name="attn_sample",
    )(page_table, q_rows, k_new, v_new, lam_p, nw, *([cache_k4] * pages), *([cache_v4] * pages))


def _merge_mlp_kernel(x_ref, ya_ref, yb_ref, yc_ref, wg_ref, wb_ref, wo_ref, l1g_ref, l1b_ref,
                      wu_ref, wd_ref, l2g_ref, l2b_ref, o_ref, *, alpha, ff_chunk):
    x = x_ref[...]
    xb = x.astype(BF16)
    merged = None
    for i, y_ref in enumerate((ya_ref, yb_ref, yc_ref)):
        gate = jax.nn.sigmoid(_nn(xb, wg_ref[:, i * D_MODEL:(i + 1) * D_MODEL]))
        term = gate * _nn(y_ref[...].astype(BF16), wb_ref[i])
        merged = term if merged is None else merged + term
    mix = _nn(merged.astype(BF16), wo_ref[...])
    x1 = _layer_norm(alpha * x + mix, l1g_ref[...], l1b_ref[...])
    x1b = x1.astype(BF16)
    acc = None
    for cidx in range(D_FF // ff_chunk):
        up = jnp.maximum(_nn(x1b, wu_ref[:, cidx * ff_chunk:(cidx + 1) * ff_chunk]), 0.0)
        term = _nn((up * up).astype(BF16), wd_ref[cidx * ff_chunk:(cidx + 1) * ff_chunk, :])
        acc = term if acc is None else acc + term
    o_ref[...] = _layer_norm(alpha * x1 + acc, l2g_ref[...], l2b_ref[...])


def _merge_mlp(x2, ya2, yb2, yc2, wl, layer, tm, alpha):
    n = x2.shape[0]
    row = lambda w: pl.BlockSpec((tm, w), lambda i: (i, 0))
    return pl.pallas_call(
        functools.partial(_merge_mlp_kernel, alpha=alpha, ff_chunk=1024),
        grid=(n // tm,),
        in_specs=[row(D_MODEL), row(BRANCH_W), row(BRANCH_W), row(BRANCH_W),
                  _weight_cols_spec(GATE_COLS, W_GATES_AT, layer), _layer_spec((N_BRANCH, BRANCH_W, D_MODEL), layer),
                  _layer_spec((D_MODEL, D_MODEL), layer), _const_spec((1, D_MODEL)), _const_spec((1, D_MODEL)),
                  _layer_spec((D_MODEL, D_FF), layer), _layer_spec((D_FF, D_MODEL), layer),
                  _const_spec((1, D_MODEL)), _const_spec((1, D_MODEL))],
        out_specs=row(D_MODEL),
        out_shape=jax.ShapeDtypeStruct((n, D_MODEL), F32),
        compiler_params=pltpu.CompilerParams(dimension_semantics=("parallel",), vmem_limit_bytes=VMEM_LIMIT),
        name="merge_mlp",
    )(x2, ya2, yb2, yc2, wl["w_all"], wl["w_branch"], wl["w_o"], wl["ln1_g"], wl["ln1_b"],
      wl["w_up"], wl["w_down"], wl["ln2_g"], wl["ln2_b"])


def _regroup_w_in(w):
    o_ab = GDN_CONV_CH + GDN_V
    o_sb = o_ab + 2 * GDN_HEADS
    o_dq = o_sb + 3 * SC_WIDTH
    wb = w.astype(BF16)
    zeros = jnp.zeros(w.shape[:2] + (W_ATT_AT - REC_COLS + LANE - 2 * GDN_HEADS,), BF16)
    w_all = jnp.concatenate([wb[..., :o_ab], wb[..., o_sb:o_dq], wb[..., o_ab:o_sb], zeros, wb[..., o_dq:]], axis=-1)
    assert w_all.shape[-1] == W_ALL_COLS
    return w_all


def _lane_row(vals, offset):
    n = vals.shape[0]
    return jnp.pad(vals.astype(F32), (offset, LANE - offset - n)).reshape(1, LANE)


def kernel(x_prompt, x_sample, cache_k, cache_v, page_table, state_gdn, state_gdn_conv, state_sc_conv, w_in,
           gdn_conv_w, gdn_a_log, gdn_dt_bias, gdn_norm_w, sc_conv_w, diff_lambda, diff_norm_w, w_branch, w_o,
           ln1_g, ln1_b, ln2_g, ln2_b, w_up, w_down):
    depth = w_in.shape[0]
    alpha = (2 * depth) ** 0.25
    b_p, seq, _ = x_prompt.shape
    b_s, dec_seq, _ = x_sample.shape
    n_phys = cache_k.shape[0]
    assert dec_seq == 4 and seq % 256 == 0
    dec_pad = SUBLANE
    rpp = PAGE_SIZE * DIFF_HEADS
    cache_k4 = cache_k.reshape(n_phys, depth, rpp, 2 * DIFF_HD)
    cache_v4 = cache_v.reshape(n_phys, depth, rpp, DIFF_VD)
    prompt_tile = 256
    tok_tile = 512
    sample_pages = 32
    assert page_table.shape[1] % sample_pages == 0

    xp = x_prompt
    xs = jnp.pad(x_sample, ((0, 0), (0, dec_pad - dec_seq), (0, 0)))
    half = (jnp.arange(2 * DIFF_HD) < DIFF_HD)
    map_mask = jnp.stack([half, ~half]).astype(F32)
    rows_p = [[], [], [], [], []]
    rows_s = [[], [], [], [], []]
    kv_prompt = None
    w_all = _regroup_w_in(w_in)
    w_branch_b, w_o_b, w_up_b, w_down_b = (w.astype(BF16) for w in (w_branch, w_o, w_up, w_down))
    for l in range(depth):
        lam_init = 0.8 - 0.6 * math.exp(-0.3 * l)
        wl = {
            "w_all": w_all, "w_branch": w_branch_b, "w_o": w_o_b,
            "ln1_g": ln1_g[l].reshape(1, D_MODEL), "ln1_b": ln1_b[l].reshape(1, D_MODEL),
            "ln2_g": ln2_g[l].reshape(1, D_MODEL), "ln2_b": ln2_b[l].reshape(1, D_MODEL),
            "w_up": w_up_b, "w_down": w_down_b,
        }
        rec_w = (gdn_conv_w[l], _lane_row(gdn_a_log[l], GDN_HEADS), _lane_row(gdn_dt_bias[l], GDN_HEADS),
                 gdn_norm_w[l].reshape(1, GDN_DV), sc_conv_w[l])
        lam_p = diff_lambda[l].astype(F32)
        nw = diff_norm_w[l].reshape(1, DIFF_VD)

        ya, yb, s_p, gbuf_p, sbuf_p = _recurrent(
            xp, w_all, jnp.zeros((b_p, GDN_HEADS, GDN_DK, GDN_DV), F32),
            jnp.zeros((b_p, GDN_CONV - 1, GDN_CONV_CH), F32), jnp.zeros((b_p, SC_CONV - 1, SC_WIDTH), F32),
            *rec_w, layer=l, nb=2, tile=prompt_tile, chunk=GDN_CHUNK, valid=prompt_tile)
        yc, *kv_prompt = _attn_prompt(xp, w_all, lam_p, nw, lam_init, prompt_tile, l, depth, kv_prompt)
        n_p = b_p * seq
        xp = _merge_mlp(xp.reshape(n_p, D_MODEL), ya.reshape(n_p, GDN_V), yb.reshape(n_p, SC_WIDTH),
                        yc.reshape(n_p, DIFF_V), wl, l, tok_tile, alpha).reshape(b_p, seq, D_MODEL)
        rows_p[2].append(s_p)
        rows_p[3].append(gbuf_p)
        rows_p[4].append(sbuf_p)

        n_s = b_s * dec_pad
        h_rec, h_att = _in_proj(xs.reshape(n_s, D_MODEL), w_all, l)
        h_rec = h_rec.reshape(b_s, dec_pad, REC_COLS)
        h_att = h_att.reshape(b_s, dec_pad, ATT_COLS)
        ya, yb, s_s, gbuf_s, sbuf_s = _recurrent(
            h_rec, None, state_gdn[:, l], state_gdn_conv[:, l], state_sc_conv[:, l],
            *rec_w, layer=l, nb=8, tile=dec_pad, chunk=dec_pad, valid=dec_seq)
        q = h_att[:, :dec_seq, 0:DIFF_QK].reshape(b_s, dec_seq, DIFF_HEADS, 2 * DIFF_HD)
        q = jnp.transpose(q, (0, 2, 1, 3))[:, :, None] * map_mask[None, None, :, None, :]
        q_rows = q.reshape(b_s, DIFF_HEADS * 2 * dec_seq, 2 * DIFF_HD)
        k_new = h_att[:, :dec_seq, DIFF_QK:2 * DIFF_QK].reshape(b_s, dec_seq * DIFF_HEADS, 2 * DIFF_HD)
        v_new = h_att[:, :dec_seq, 2 * DIFF_QK:ATT_COLS].reshape(b_s, dec_seq * DIFF_HEADS, DIFF_VD)
        o = _attn_sample(q_rows, jnp.pad(k_new, ((0, 0), (0, LANE - dec_seq * DIFF_HEADS), (0, 0))),
                         jnp.pad(v_new, ((0, 0), (0, LANE - dec_seq * DIFF_HEADS), (0, 0))),
                         lam_p, nw, cache_k4, cache_v4, page_table, l, lam_init, pages=sample_pages)
        yc = jnp.transpose(o[:, :, :dec_seq, :], (0, 2, 1, 3)).reshape(b_s, dec_seq, DIFF_V)
        yc = jnp.pad(yc, ((0, 0), (0, dec_pad - dec_seq), (0, 0)))
        xs = _merge_mlp(xs.reshape(n_s, D_MODEL), ya.reshape(n_s, GDN_V), yb.reshape(n_s, SC_WIDTH),
                        yc.reshape(n_s, DIFF_V), wl, l, n_s, alpha).reshape(b_s, dec_pad, D_MODEL)
        rows_s[0].append(k_new.reshape(b_s, dec_seq, DIFF_HEADS, 2 * DIFF_HD))
        rows_s[1].append(v_new.reshape(b_s, dec_seq, DIFF_HEADS, DIFF_VD))
        rows_s[2].append(s_s)
        rows_s[3].append(gbuf_s)
        rows_s[4].append(sbuf_s)

    outs_p = [a.reshape(b_p, depth, seq, DIFF_HEADS, DIFF_VD) for a in kv_prompt]
    outs_p += [jnp.stack(r, axis=1) for r in rows_p[2:]]
    outs_s = [jnp.stack(r, axis=1) for r in rows_s]
    return (xp, xs[:, :dec_seq], *outs_p, *outs_s)
```
